```python
import math
import jax
import jax.numpy as jnp
from jax import lax
import numpy as np

D_MODEL = 1024
BATCH = 2
SEQ = 8192
DEPTH = 4

GRID_W = 64
CTX_LEN = 256
ROPE_THETA = 10000.0
NORM_EPS = 1e-6
Q_BLOCK = 128

A_HEADS = 4
A_HD = 64
A_VD = 2 * A_HD
A_SCALE = A_HD ** -0.5
A_PROJ = 4 * A_HEADS * A_HD + A_HEADS * A_VD

B_HEADS = 4
B_NOPE = 128
B_ROPE = 64
B_VD = 128
B_Q_RANK = 256
B_KV_RANK = 128
B_SCALE = (B_NOPE + B_ROPE) ** -0.5
B_PROJ = B_Q_RANK + B_KV_RANK + B_ROPE

AB_PROJ = A_PROJ + B_PROJ
AB_OUT = A_HEADS * A_VD + B_HEADS * B_VD

C_HEADS = 8
C_HD = D_MODEL // C_HEADS
C_WIDTH = C_HEADS * C_HD
C_PROJ = 5 * C_WIDTH
C_CHUNK = 64

N_EXPERTS = 32
TOP_K = 4
D_FF = D_MODEL
SWIGLU_ALPHA = 1.702
SWIGLU_LIMIT = 7.0
MOE_BLOCK = 128

N_AB_LAYERS = (DEPTH + 1) // 2
N_C_LAYERS = DEPTH // 2

kernel_name = 'hybrid_diff_mla_hgrn2_moe'


def rms_norm(x, g):
    xf = x.astype(jnp.float32)
    y = xf * lax.rsqrt(jnp.mean(xf * xf, axis=-1, keepdims=True) + NORM_EPS)
    return (y * g.astype(jnp.float32)).astype(x.dtype)


def axial_rope_tables(rows, rot_dim):
    quarter = rot_dim // 4
    inv_freq = ROPE_THETA ** (-jnp.arange(quarter, dtype=jnp.float32) / quarter)
    t = jnp.arange(rows * GRID_W)
    row = (t // GRID_W).astype(jnp.float32)
    col = (t % GRID_W).astype(jnp.float32)
    ang_r = row[:, None] * inv_freq
    ang_c = col[:, None] * inv_freq
    return (jnp.cos(ang_r), jnp.sin(ang_r), jnp.cos(ang_c), jnp.sin(ang_c))


def _rotate(x, cos, sin):
    x1, x2 = jnp.split(x, 2, axis=-1)
    return jnp.concatenate([x1 * cos - x2 * sin, x2 * cos + x1 * sin], axis=-1)


def apply_axial_rope(x, tables):
    cr, sr, cc, sc = tables
    xr, xc = jnp.split(x, 2, axis=-1)
    return jnp.concatenate([_rotate(xr, cr, sr), _rotate(xc, cc, sc)], axis=-1).astype(x.dtype)


def merge_heads(o):
    b, h, t, d = o.shape
    return o.transpose(0, 2, 1, 3).reshape(b, t, h * d)


def context_attention(q, k, v, scale):
    s = jnp.einsum('bhqd,bhkd->bhqk', q, k, preferred_element_type=jnp.float32) * scale
    p = jax.nn.softmax(s, axis=-1)
    return jnp.einsum('bhqk,bhkd->bhqd', p.astype(v.dtype), v)


def latent_attention(q, k_lat, v_lat, k_ctx, v_ctx, scale):
    k = jnp.concatenate([k_ctx, k_lat], axis=2)
    v = jnp.concatenate([v_ctx, v_lat], axis=2)
    b, h, n, dk = q.shape
    nb = n // Q_BLOCK
    qb = jnp.moveaxis(q.reshape(b, h, nb, Q_BLOCK, dk), 2, 0)
    ob = lax.map(lambda qi: context_attention(qi, k, v, scale), qb)
    return jnp.moveaxis(ob, 0, 2).reshape(b, h, n, v.shape[-1])


def diff_heads(p, rope):
    b, t, _ = p.shape
    q, k, v = jnp.split(p, [2 * A_HEADS * A_HD, 4 * A_HEADS * A_HD], axis=-1)
    q = q.reshape(b, t, 2 * A_HEADS, A_HD).transpose(0, 2, 1, 3)
    k = k.reshape(b, t, 2 * A_HEADS, A_HD).transpose(0, 2, 1, 3)
    v = v.reshape(b, t, A_HEADS, A_VD).transpose(0, 2, 1, 3)
    v = jnp.repeat(v, 2, axis=1)
    if rope is not None:
        q = apply_axial_rope(q, rope)
        k = apply_axial_rope(k, rope)
    return q, k, v


def diff_combine(o2, lam, lam_init, subln_g):
    b, _, t, vd = o2.shape
    o2 = o2.reshape(b, A_HEADS, 2, t, vd)
    o = o2[:, :, 0] - lam * o2[:, :, 1]
    o = rms_norm(o, subln_g) * (1.0 - lam_init)
    return merge_heads(o)


def mla_heads(p, q_norm_g, kv_norm_g, w_uq, w_ukv, rope):
    b, t, _ = p.shape
    c_q, c_kv, k_r = jnp.split(p, [B_Q_RANK, B_Q_RANK + B_KV_RANK], axis=-1)
    q = (rms_norm(c_q, q_norm_g) @ w_uq).reshape(b, t, B_HEADS, B_NOPE + B_ROPE).transpose(0, 2, 1, 3)
    kv = (rms_norm(c_kv, kv_norm_g) @ w_ukv).reshape(b, t, B_HEADS, B_NOPE + B_VD).transpose(0, 2, 1, 3)
    q_nope, q_r = jnp.split(q, [B_NOPE], axis=-1)
    k_nope, v = jnp.split(kv, [B_NOPE], axis=-1)
    k_r = k_r[:, None]
    if rope is not None:
        q_r = apply_axial_rope(q_r, rope)
        k_r = apply_axial_rope(k_r, rope)
    q = jnp.concatenate([q_nope, q_r], axis=-1)
    k = jnp.concatenate([k_nope, jnp.broadcast_to(k_r, (b, B_HEADS, t, B_ROPE))], axis=-1)
    return q, k, v


def ab_mixer(u_lat, u_ctx, w_in, diff_lambda, subln_g, q_norm_g, kv_norm_g, w_uq, w_ukv, w_out,
             lam_init, rope_a, rope_b, with_ctx_out):
    p_lat = u_lat @ w_in
    p_ctx = u_ctx @ w_in
    lf = diff_lambda.astype(jnp.float32)
    lam = jnp.exp(jnp.sum(lf[0] * lf[1])) - jnp.exp(jnp.sum(lf[2] * lf[3])) + lam_init
    qa, ka, va = diff_heads(p_lat[..., :A_PROJ], rope_a)
    qac, kac, vac = diff_heads(p_ctx[..., :A_PROJ], None)
    qb, kb, vb = mla_heads(p_lat[..., A_PROJ:], q_norm_g, kv_norm_g, w_uq, w_ukv, rope_b)
    qbc, kbc, vbc = mla_heads(p_ctx[..., A_PROJ:], q_norm_g, kv_norm_g, w_uq, w_ukv, None)
    oa = diff_combine(latent_attention(qa, ka, va, kac, vac, A_SCALE), lam, lam_init, subln_g)
    ob = merge_heads(latent_attention(qb, kb, vb, kbc, vbc, B_SCALE))
    y_lat = jnp.concatenate([oa, ob], axis=-1).astype(u_lat.dtype) @ w_out
    y_ctx = None
    if with_ctx_out:
        oac = diff_combine(context_attention(qac, kac, vac, A_SCALE), lam, lam_init, subln_g)
        obc = merge_heads(context_attention(qbc, kbc, vbc, B_SCALE))
        y_ctx = jnp.concatenate([oac, obc], axis=-1).astype(u_ctx.dtype) @ w_out
    return y_lat, y_ctx


def gla_chunk_scan(q, k, v, log_f, s0):
    b, h, t, dk = q.shape
    dv = v.shape[-1]
    n = t // C_CHUNK

    def to_chunks(a):
        return jnp.moveaxis(a.reshape(b, h, n, C_CHUNK, a.shape[-1]), 2, 0)

    mask = jnp.tril(jnp.ones((C_CHUNK, C_CHUNK), dtype=bool))[:, :, None]

    def step(s, inp):
        qc, kc, vc, lf = inp
        cum = jnp.cumsum(lf, axis=2)
        rel = jnp.where(mask, cum[:, :, :, None, :] - cum[:, :, None, :, :], -jnp.inf)
        att = jnp.einsum('bhtd,bhsd,bhtsd->bhts', qc, kc, jnp.exp(rel))
        o = jnp.einsum('bhts,bhse->bhte', att, vc) + jnp.einsum('bhtd,bhde->bhte', qc * jnp.exp(cum), s)
        last = cum[:, :, -1]
        s_new = jnp.exp(last)[..., None] * s + jnp.einsum(
            'bhsd,bhse->bhde', kc * jnp.exp(last[:, :, None, :] - cum), vc)
        return s_new, o

    s_fin, o = lax.scan(step, s0, (to_chunks(q), to_chunks(k), to_chunks(v), to_chunks(log_f)))
    return jnp.moveaxis(o, 0, 2).reshape(b, h, t, dv), s_fin


def hgrn2_mixer(u_lat, u_ctx, w_in, lb, norm_g, w_out, with_ctx_out):
    lbh = lb.astype(jnp.float32).reshape(C_HEADS, 1, C_HD)

    def project(u):
        b, t, _ = u.shape
        q, zf, zb, i, g = jnp.split(u @ w_in, 5, axis=-1)

        def heads(a):
            return a.reshape(b, t, C_HEADS, C_HD).transpose(0, 2, 1, 3)

        def gate(z):
            f = lbh + (1.0 - lbh) * jax.nn.sigmoid(heads(z).astype(jnp.float32))
            return 1.0 - f, jnp.log(f)

        return heads(q), heads(i), gate(zf), gate(zb), g

    def readout(o, g):
        b, _, t, _ = o.shape
        o = rms_norm(o, norm_g).transpose(0, 2, 1, 3).reshape(b, t, C_WIDTH)
        return (o * jax.nn.silu(g.astype(jnp.float32))).astype(g.dtype) @ w_out

    def flip(a):
        return jnp.flip(a, axis=2)

    q_c, v_c, (kf_c, lf_c), (kb_c, lgb_c), g_c = project(u_ctx)
    s0 = jnp.zeros((u_ctx.shape[0], C_HEADS, C_HD, C_HD), jnp.float32)
    o_cf, s_f = gla_chunk_scan(q_c, kf_c, v_c, lf_c, s0)
    o_cb, s_b = gla_chunk_scan(flip(q_c), flip(kb_c), flip(v_c), flip(lgb_c), s0)
    q_l, v_l, (kf_l, lf_l), (kb_l, lgb_l), g_l = project(u_lat)
    o_lf, _ = gla_chunk_scan(q_l, kf_l, v_l, lf_l, s_f)
    o_lb, _ = gla_chunk_scan(flip(q_l), flip(kb_l), flip(v_l), flip(lgb_l), s_b)
    y_lat = readout(o_lf + flip(o_lb), g_l)
    y_ctx = readout(o_cf + flip(o_cb), g_c) if with_ctx_out else None
    return y_lat, y_ctx


def expert_ffn(xb, w1, b1, w2, b2):
    h = jnp.dot(xb, w1) + b1
    x_glu = jnp.minimum(h[:, 0::2], SWIGLU_LIMIT)
    x_lin = jnp.clip(h[:, 1::2], -SWIGLU_LIMIT, SWIGLU_LIMIT)
    y = x_glu * jax.nn.sigmoid(SWIGLU_ALPHA * x_glu) * (x_lin + 1.0)
    return jnp.dot(y, w2) + b2


def moe_ffn(tok, w_router, b_router, w1, b1, w2, b2):
    t, d = tok.shape
    logits = jnp.einsum('td,de->te', tok, w_router, preferred_element_type=jnp.float32) + b_router.astype(jnp.float32)
    top_val, top_idx = lax.top_k(logits, TOP_K)
    gates = jax.nn.softmax(top_val, axis=-1)
    n_assign = t * TOP_K
    flat_e = top_idx.reshape(-1).astype(jnp.int32)
    flat_tok = jnp.arange(n_assign, dtype=jnp.int32) // TOP_K
    flat_g = gates.reshape(-1)
    order = jnp.argsort(flat_e)
    e_sorted = flat_e[order]
    counts = jnp.bincount(flat_e, length=N_EXPERTS).astype(jnp.int32)
    group_start = jnp.cumsum(counts) - counts
    padded = (counts + MOE_BLOCK - 1) // MOE_BLOCK * MOE_BLOCK
    pad_end = jnp.cumsum(padded)
    pad_start = pad_end - padded
    dest = pad_start[e_sorted] + (jnp.arange(n_assign, dtype=jnp.int32) - group_start[e_sorted])
    n_blocks = -(-n_assign // MOE_BLOCK) + N_EXPERTS
    n_rows = n_blocks * MOE_BLOCK
    row_tok = jnp.full((n_rows,), t, jnp.int32).at[dest].set(flat_tok[order])
    row_gate = jnp.zeros((n_rows,), jnp.float32).at[dest].set(flat_g[order])
    block_start = jnp.arange(n_blocks, dtype=jnp.int32) * MOE_BLOCK
    block_expert = jnp.minimum(jnp.searchsorted(pad_end, block_start, side='right'), N_EXPERTS - 1)
    tok_pad = jnp.concatenate([tok, jnp.zeros((1, d), tok.dtype)], axis=0)
    xb = tok_pad[row_tok].reshape(n_blocks, MOE_BLOCK, d)
    yb = lax.map(lambda a: expert_ffn(a[0], w1[a[1]], b1[a[1]], w2[a[1]], b2[a[1]]), (xb, block_expert))
    y = yb.reshape(n_rows, d) * row_gate[:, None].astype(yb.dtype)
    return jnp.zeros((t + 1, d), y.dtype).at[row_tok].add(y)[:t]


def setup_inputs(seed: int = 0) -> dict:
    key = jax.random.key(seed)
    keys = iter(jax.random.split(key, 40))

    def nrm(shape, scale):
        return jax.random.normal(next(keys), shape, jnp.float32) * scale

    def gain(shape):
        return 1.0 + nrm(shape, 0.05)

    D = D_MODEL
    return {
        'x': nrm((BATCH, SEQ, D), 1.0),
        'c': nrm((BATCH, D), 1.0),
        'ctx': nrm((BATCH, CTX_LEN, D), 1.0),
        'c_ctx': nrm((D,), 1.0),
        'norm_mix_g': gain((DEPTH, D)),
        'norm_ffn_g': gain((DEPTH, D)),
        'w_ada': nrm((DEPTH, D, 6 * D), 0.3 * D ** -0.5),
        'b_ada': nrm((DEPTH, 6 * D), 0.02),
        'w_in_ab': nrm((N_AB_LAYERS, D, AB_PROJ), D ** -0.5),
        'diff_lambda': nrm((N_AB_LAYERS, 4, A_HD), 0.1),
        'diff_subln_g': gain((N_AB_LAYERS, A_VD)),
        'mla_q_norm_g': gain((N_AB_LAYERS, B_Q_RANK)),
        'mla_kv_norm_g': gain((N_AB_LAYERS, B_KV_RANK)),
        'w_uq': nrm((N_AB_LAYERS, B_Q_RANK, B_HEADS * (B_NOPE + B_ROPE)), B_Q_RANK ** -0.5),
        'w_ukv': nrm((N_AB_LAYERS, B_KV_RANK, B_HEADS * (B_NOPE + B_VD)), B_KV_RANK ** -0.5),
        'w_out_ab': nrm((N_AB_LAYERS, AB_OUT, D), AB_OUT ** -0.5),
        'w_in_c': nrm((N_C_LAYERS, D, C_PROJ), D ** -0.5),
        'lb_raw': nrm((DEPTH, C_WIDTH), 0.1),
        'hgrn_norm_g': gain((N_C_LAYERS, C_HD)),
        'w_out_c': nrm((N_C_LAYERS, C_WIDTH, D), C_WIDTH ** -0.5),
        'w_router': nrm((DEPTH, D, N_EXPERTS), D ** -0.5),
        'b_router': nrm((DEPTH, N_EXPERTS), 0.01),
        'w_exp1': nrm((DEPTH, N_EXPERTS, D, 2 * D_FF), D ** -0.5),
        'b_exp1': nrm((DEPTH, N_EXPERTS, 2 * D_FF), 0.01),
        'w_exp2': nrm((DEPTH, N_EXPERTS, D_FF, D), D_FF ** -0.5),
        'b_exp2': nrm((DEPTH, N_EXPERTS, D), 0.01),
        'final_g': gain((D,)),
    }


def reference(x, c, ctx, c_ctx, norm_mix_g, norm_ffn_g, w_ada, b_ada, w_in_ab, diff_lambda, diff_subln_g,
              mla_q_norm_g, mla_kv_norm_g, w_uq, w_ukv, w_out_ab, w_in_c, lb_raw, hgrn_norm_g, w_out_c,
              w_router, b_router, w_exp1, b_exp1, w_exp2, b_exp2, final_g):
    b, n, d = x.shape
    rows = n // GRID_W
    rope_a = axial_rope_tables(rows, A_HD)
    rope_b = axial_rope_tables(rows, B_ROPE)
    lb_p = jax.nn.softmax(lb_raw.astype(jnp.float32), axis=0)
    lower_bounds = jnp.cumsum(lb_p, axis=0) - lb_p[0]
    silu_c = jax.nn.silu(c)
    silu_cc = jax.nn.silu(c_ctx)
    h, hc = x, ctx
    for l in range(DEPTH):
        last = l == DEPTH - 1
        sh1, sc1, g1, sh2, sc2, g2 = jnp.split((silu_c @ w_ada[l] + b_ada[l])[:, None, :], 6, axis=-1)
        csh1, csc1, cg1, csh2, csc2, cg2 = jnp.split(silu_cc @ w_ada[l] + b_ada[l], 6, axis=-1)
        u = rms_norm(h, norm_mix_g[l]) * (1.0 + sc1) + sh1
        uc = rms_norm(hc, norm_mix_g[l]) * (1.0 + csc1) + csh1
        j = l // 2
        if l % 2 == 0:
            lam_init = 0.8 - 0.6 * math.exp(-0.3 * l)
            y, yc = ab_mixer(u, uc, w_in_ab[j], diff_lambda[j], diff_subln_g[j], mla_q_norm_g[j],
                             mla_kv_norm_g[j], w_uq[j], w_ukv[j], w_out_ab[j], lam_init, rope_a, rope_b,
                             not last)
        else:
            y, yc = hgrn2_mixer(u, uc, w_in_c[j], lower_bounds[l], hgrn_norm_g[j], w_out_c[j], not last)
        h = h + g1 * y
        v = rms_norm(h, norm_ffn_g[l]) * (1.0 + sc2) + sh2
        moe_w = (w_router[l], b_router[l], w_exp1[l], b_exp1[l], w_exp2[l], b_exp2[l])
        if last:
            h = h + g2 * moe_ffn(v.reshape(b * n, d), *moe_w).reshape(b, n, d)
        else:
            hc = hc + cg1 * yc
            vc = rms_norm(hc, norm_ffn_g[l]) * (1.0 + csc2) + csh2
            out = moe_ffn(jnp.concatenate([v.reshape(b * n, d), vc.reshape(-1, d)], axis=0), *moe_w)
            h = h + g2 * out[:b * n].reshape(b, n, d)
            hc = hc + cg2 * out[b * n:].reshape(hc.shape)
    return rms_norm(h, final_g)
```

```python
import functools
import math

import jax
import jax.numpy as jnp
from jax import lax
from jax.experimental import pallas as pl
from jax.experimental.pallas import tpu as pltpu

F32 = jnp.float32
BF16 = jnp.bfloat16

D_MODEL = 1024
DEPTH = 4
GRID_W = 64
ROPE_THETA = 10000.0
NORM_EPS = 1e-6

A_HEADS = 4
A_HD = 64
A_VD = 128
A_SCALE = A_HD ** -0.5
B_HEADS = 4
B_NOPE = 128
B_ROPE = 64
B_VD = 128
B_Q_RANK = 256
B_KV_RANK = 128
B_SCALE = (B_NOPE + B_ROPE) ** -0.5
AB_PROJ_PAD = 2048

C_HEADS = 8
C_HD = 128
C_CHUNK = 64
C_SUB = 16

N_EXPERTS = 32
TOP_K = 4
SWIGLU_ALPHA = 1.702
SWIGLU_LIMIT = 7.0

LANES = 128
ROW_TILE = 512
PREP_TILE = 256
MOE_TILE = 256
MOD_ROWS = 8


def _split_hi_lo(x):
    hi = x.astype(BF16)
    lo = (x - hi.astype(F32)).astype(BF16)
    return hi, lo


def _dot(a, b):
    return jnp.dot(a, b, preferred_element_type=F32)


def _dot_nt(a, b):
    return lax.dot_general(a, b, (((1,), (1,)), ((), ())), preferred_element_type=F32)


def _dot_tn(a, b):
    return lax.dot_general(a, b, (((0,), (0,)), ((), ())), preferred_element_type=F32)


def _rms(x, g):
    ms = jnp.mean(x * x, axis=-1, keepdims=True)
    return x * lax.rsqrt(ms + NORM_EPS) * g


def _ada_kernel(c_ref, w_ref, b_ref, o_ref):
    c = c_ref[...]
    s = c * jax.nn.sigmoid(c)
    s_hi, s_lo = _split_hi_lo(s)
    w_hi, w_lo = _split_hi_lo(w_ref[...])
    o_ref[...] = _dot(s_hi, w_hi) + _dot(s_lo, w_hi) + _dot(s_hi, w_lo) + b_ref[...]


def _ada(cvec, w_ada, b_ada):
    depth, d, n6 = w_ada.shape
    tn = 1536
    return pl.pallas_call(
        _ada_kernel,
        grid=(depth, n6 // tn),
        in_specs=[
            pl.BlockSpec((MOD_ROWS, d), lambda l, j: (0, 0)),
            pl.BlockSpec((None, d, tn), lambda l, j: (l, 0, j)),
            pl.BlockSpec((None, 1, tn), lambda l, j: (l, 0, j)),
        ],
        out_specs=pl.BlockSpec((None, MOD_ROWS, tn), lambda l, j: (l, 0, j)),
        out_shape=jax.ShapeDtypeStruct((depth, MOD_ROWS, n6), F32),
        name="ada_mod",
    )(cvec, w_ada, b_ada.reshape(depth, 1, n6))


def _inproj_kernel(h_ref, g_ref, sc_ref, sh_ref, w_ref, o_ref, u_scr):
    @pl.when(pl.program_id(1) == 0)
    def _():
        u = _rms(h_ref[...], g_ref[...]) * (1.0 + sc_ref[...]) + sh_ref[...]
        u_scr[...] = u.astype(BF16)

    o_ref[...] = _dot(u_scr[...], w_ref[...])


def _mod_spec(n_lat_blocks):
    def idx(i, *_):
        return (jnp.minimum(i // n_lat_blocks, 2), 0, 0)

    return pl.BlockSpec((None, 1, D_MODEL), idx)


def _inproj(h_all, gain, sc, sh, w_bf, n_lat, tn):
    t_all, d = h_all.shape
    nout = w_bf.shape[1]
    tm = ROW_TILE
    mod = _mod_spec(n_lat // tm)
    return pl.pallas_call(
        _inproj_kernel,
        grid=(t_all // tm, nout // tn),
        in_specs=[
            pl.BlockSpec((tm, d), lambda i, j: (i, 0)),
            pl.BlockSpec((1, d), lambda i, j: (0, 0)),
            mod,
            mod,
            pl.BlockSpec((d, tn), lambda i, j: (0, j)),
        ],
        out_specs=pl.BlockSpec((tm, tn), lambda i, j: (i, j)),
        out_shape=jax.ShapeDtypeStruct((t_all, nout), F32),
        scratch_shapes=[pltpu.VMEM((tm, d), BF16)],
        compiler_params=pltpu.CompilerParams(dimension_semantics=("arbitrary", "arbitrary")),
        name="inproj",
    )(h_all, gain.reshape(1, d), sc, sh, w_bf)


def _rope(x, cos, sin):
    n = x.shape[-1]
    lane = lax.broadcasted_iota(jnp.int32, x.shape, 1)
    first = (lane // 16) % 2 == 0
    partner = jnp.where(first, pltpu.roll(x, n - 16, 1), pltpu.roll(x, 16, 1))
    return x * cos + partner * sin


def _abprep_kernel(p_ref, cosa_ref, sina_ref, cosb_ref, sinb_ref, qg_ref, kvg_ref, wuq_ref, wuk_ref, wuv_ref,
                   qa_ref, qb_ref, ka_ref, va_ref, kb_ref, vb_ref):
    cosa, sina = cosa_ref[...], sina_ref[...]
    cosb, sinb = cosb_ref[...], sinb_ref[...]
    n_a = 2 * A_HEADS * A_HD
    qa = [_rope(p_ref[:, c:c + LANES], cosa, sina) * A_SCALE for c in range(0, n_a, LANES)]
    qa_ref[...] = jnp.concatenate(qa, axis=1).astype(BF16)
    ka = [_rope(p_ref[:, n_a + c:n_a + c + LANES], cosa, sina) for c in range(0, n_a, LANES)]
    ka_ref[...] = jnp.concatenate(ka, axis=1).astype(BF16)
    va_ref[...] = p_ref[:, 2 * n_a:2 * n_a + A_HEADS * A_VD].astype(BF16)

    off = 2 * n_a + A_HEADS * A_VD
    cq = _rms(p_ref[:, off:off + B_Q_RANK], qg_ref[...]).astype(BF16)
    qf = _dot(cq, wuq_ref[...])
    ckv = _rms(p_ref[:, off + B_Q_RANK:off + B_Q_RANK + B_KV_RANK], kvg_ref[...]).astype(BF16)
    kn = _dot(ckv, wuk_ref[...])
    vb_ref[...] = _dot(ckv, wuv_ref[...]).astype(BF16)
    kr_off = off + B_Q_RANK + B_KV_RANK
    krr = _rope(p_ref[:, kr_off:kr_off + LANES], cosb, sinb)
    qb, kb = [], []
    for h in range(B_HEADS):
        qb.append(qf[:, 2 * LANES * h:2 * LANES * h + LANES] * B_SCALE)
        qb.append(_rope(qf[:, 2 * LANES * h + LANES:2 * LANES * (h + 1)], cosb, sinb) * B_SCALE)
        kb.append(kn[:, LANES * h:LANES * (h + 1)])
        kb.append(krr)
    qb_ref[...] = jnp.concatenate(qb, axis=1).astype(BF16)
    kb_ref[...] = jnp.concatenate(kb, axis=1).astype(BF16)


def _abprep(p, tabs, qg, kvg, wuq, wuk, wuv, n_lat, n_ctx, batch):
    t_all = p.shape[0]
    tm = PREP_TILE
    nlb = n_lat // tm
    ncb = n_ctx // tm
    kvb = nlb + ncb

    def tab_idx(i):
        return (jnp.where(i < batch * nlb, i % nlb, nlb), 0)

    def kv_idx(i):
        lat = (i // nlb) * kvb + ncb + i % nlb
        j = i - batch * nlb
        ctx = (j // ncb) * kvb + j % ncb
        return (jnp.where(i < batch * nlb, lat, ctx), 0)

    tab_spec = pl.BlockSpec((tm, LANES), tab_idx)
    full = lambda a: pl.BlockSpec(a.shape, lambda i: (0,) * a.ndim)
    wq, wk, wv = 2 * A_HEADS * A_HD, B_HEADS * 2 * LANES, A_HEADS * A_VD
    kv_rows = batch * (n_lat + n_ctx)
    return pl.pallas_call(
        _abprep_kernel,
        grid=(t_all // tm,),
        in_specs=[pl.BlockSpec((tm, AB_PROJ_PAD), lambda i: (i, 0)), tab_spec, tab_spec, tab_spec, tab_spec,
                  full(qg), full(kvg), full(wuq), full(wuk), full(wuv)],
        out_specs=[
            pl.BlockSpec((tm, wq), lambda i: (i, 0)),
            pl.BlockSpec((tm, wk), lambda i: (i, 0)),
            pl.BlockSpec((tm, wq), kv_idx),
            pl.BlockSpec((tm, wv), kv_idx),
            pl.BlockSpec((tm, wk), kv_idx),
            pl.BlockSpec((tm, wv), kv_idx),
        ],
        out_shape=[
            jax.ShapeDtypeStruct((t_all, wq), BF16),
            jax.ShapeDtypeStruct((t_all, wk), BF16),
            jax.ShapeDtypeStruct((kv_rows, wq), BF16),
            jax.ShapeDtypeStruct((kv_rows, wv), BF16),
            jax.ShapeDtypeStruct((kv_rows, wk), BF16),
            jax.ShapeDtypeStruct((kv_rows, wv), BF16),
        ],
        name="ab_prep",
    )(p, *tabs, qg, kvg, wuq, wuk, wuv)


def _softmax_step(s, v, m_ref, l_ref, acc_ref, idx):
    m_prev = m_ref[idx]
    m_new = jnp.maximum(m_prev, jnp.max(s, axis=-1, keepdims=True))
    alpha = jnp.exp(m_prev - m_new)
    p = jnp.exp(s - m_new)
    l_ref[idx] = alpha * l_ref[idx] + jnp.sum(p, axis=-1, keepdims=True)
    acc_ref[idx] = alpha * acc_ref[idx] + _dot(p.astype(BF16), v)
    m_ref[idx] = m_new


def _flash_init(m_ref, l_ref, acc_ref):
    m_ref[...] = jnp.full(m_ref.shape, -jnp.inf, F32)
    l_ref[...] = jnp.zeros(l_ref.shape, F32)
    acc_ref[...] = jnp.zeros(acc_ref.shape, F32)


def _flash_diff_kernel(q_ref, k_ref, v_ref, lam_ref, g_ref, o_ref, m_ref, l_ref, acc_ref, *, lam_init):
    j = pl.program_id(3)

    @pl.when(j == 0)
    def _():
        _flash_init(m_ref, l_ref, acc_ref)

    q = q_ref[...]
    k = k_ref[...]
    v = v_ref[...]
    lane = lax.broadcasted_iota(jnp.int32, q.shape, 1)
    zero = jnp.zeros_like(q)
    _softmax_step(_dot_nt(jnp.where(lane < A_HD, q, zero), k), v, m_ref, l_ref, acc_ref, 0)
    _softmax_step(_dot_nt(jnp.where(lane >= A_HD, q, zero), k), v, m_ref, l_ref, acc_ref, 1)

    @pl.when(j == pl.num_programs(3) - 1)
    def _():
        lf = lam_ref[...]
        lam = (jnp.exp(jnp.sum(lf[0:1] * lf[1:2], axis=-1, keepdims=True))
               - jnp.exp(jnp.sum(lf[2:3] * lf[3:4], axis=-1, keepdims=True)) + lam_init)
        o = acc_ref[0] / l_ref[0] - lam * (acc_ref[1] / l_ref[1])
        o_ref[...] = (_rms(o, g_ref[...]) * (1.0 - lam_init)).astype(o_ref.dtype)


def _flash_mla_kernel(q_ref, k_ref, v_ref, o_ref, m_ref, l_ref, acc_ref):
    j = pl.program_id(3)

    @pl.when(j == 0)
    def _():
        _flash_init(m_ref, l_ref, acc_ref)

    _softmax_step(_dot_nt(q_ref[...], k_ref[...]), v_ref[...], m_ref, l_ref, acc_ref, 0)

    @pl.when(j == pl.num_programs(3) - 1)
    def _():
        o_ref[...] = (acc_ref[0] / l_ref[0]).astype(o_ref.dtype)


def _flash(kind, q, k, v, extra, *, batch, heads, tq, tk, q_blk0, q_blk_stride, nq, kv_blk_stride, nkv, lam_init=0.0):
    wq = q.shape[1] // heads
    wv = v.shape[1] // heads
    n_maps = 2 if kind == "diff" else 1
    q_spec = pl.BlockSpec((tq, wq), lambda b, h, i, j: (q_blk0 + b * q_blk_stride + i, h))
    k_spec = pl.BlockSpec((tk, wq), lambda b, h, i, j: (b * kv_blk_stride + j, h))
    v_spec = pl.BlockSpec((tk, wv), lambda b, h, i, j: (b * kv_blk_stride + j, h))
    o_spec = pl.BlockSpec((tq, wv), lambda b, h, i, j: (q_blk0 + b * q_blk_stride + i, h))
    in_specs = [q_spec, k_spec, v_spec]
    if kind == "diff":
        body = functools.partial(_flash_diff_kernel, lam_init=lam_init)
        in_specs += [pl.BlockSpec(e.shape, lambda b, h, i, j: (0, 0)) for e in extra]
    else:
        body = _flash_mla_kernel
    return pl.pallas_call(
        body,
        grid=(batch, heads, nq, nkv),
        in_specs=in_specs,
        out_specs=o_spec,
        out_shape=jax.ShapeDtypeStruct((q.shape[0], heads * wv), BF16),
        scratch_shapes=[
            pltpu.VMEM((n_maps, tq, 1), F32),
            pltpu.VMEM((n_maps, tq, 1), F32),
            pltpu.VMEM((n_maps, tq, wv), F32),
        ],
        compiler_params=pltpu.CompilerParams(
            dimension_semantics=("arbitrary", "arbitrary", "arbitrary", "arbitrary")),
        name="flash_" + kind,
    )(q, k, v, *extra)


def _out_epilogue(y, h_ref, g1_ref, nf_ref, sc2_ref, sh2_ref, wrh_ref, wrl_ref, br_ref, hn_ref, v_ref, lg_ref):
    hn = h_ref[...] + g1_ref[...] * y
    hn_ref[...] = hn
    v = _rms(hn, nf_ref[...]) * (1.0 + sc2_ref[...]) + sh2_ref[...]
    v_hi, v_lo = _split_hi_lo(v)
    v_ref[...] = v_hi
    wrh = wrh_ref[...]
    lg_ref[...] = _dot(v_hi, wrh) + _dot(v_lo, wrh) + _dot(v_hi, wrl_ref[...]) + br_ref[...]


def _about_kernel(oa_ref, ob_ref, wa_ref, wb_ref, *rest):
    y = _dot(oa_ref[...], wa_ref[...]) + _dot(ob_ref[...], wb_ref[...])
    _out_epilogue(y, *rest)


def _cout_kernel(of_ref, obk_ref, gate_ref, ng_ref, w_ref, *rest):
    o = of_ref[...] + obk_ref[...]
    ng = ng_ref[...]
    parts = [_rms(o[:, c:c + C_HD], ng) for c in range(0, C_HEADS * C_HD, C_HD)]
    g = gate_ref[...]
    x = jnp.concatenate(parts, axis=1) * (g * jax.nn.sigmoid(g))
    _out_epilogue(_dot(x.astype(BF16), w_ref[...]), *rest)


def _outproj(kind, ins, in_specs, h_all, g1, nf, sc2, sh2, wrh, wrl, br, n_lat):
    t_all, d = h_all.shape
    tm = ROW_TILE
    mod = _mod_spec(n_lat // tm)
    row = lambda w: pl.BlockSpec((tm, w), lambda i: (i, 0))
    full = lambda a: pl.BlockSpec(a.shape, lambda i: (0,) * a.ndim)
    nf = nf.reshape(1, d)
    return pl.pallas_call(
        _about_kernel if kind == "ab" else _cout_kernel,
        grid=(t_all // tm,),
        in_specs=in_specs + [row(d), mod, full(nf), mod, mod, full(wrh), full(wrl), full(br)],
        out_specs=[row(d), row(d), row(LANES)],
        out_shape=[
            jax.ShapeDtypeStruct((t_all, d), F32),
            jax.ShapeDtypeStruct((t_all, d), BF16),
            jax.ShapeDtypeStruct((t_all, LANES), F32),
        ],
        name="outproj_" + kind,
    )(*ins, h_all, g1, nf, sc2, sh2, wrh, wrl, br)


def _cumsum_rows(tri_bf, x):
    hi = x.astype(BF16)
    r1 = x - hi.astype(F32)
    mid = r1.astype(BF16)
    lo = (r1 - mid.astype(F32)).astype(BF16)
    return _dot(tri_bf, hi) + _dot(tri_bf, mid) + _dot(tri_bf, lo)


def _scan_kernel(q_ref, z_ref, v_ref, lb_ref, o_ref, st_ref, *, reverse):
    c = pl.program_id(1)

    @pl.when(c == 0)
    def _():
        st_ref[...] = jnp.zeros(st_ref.shape, F32)

    L, SB = C_CHUNK, C_SUB
    nsb = L // SB
    lb = lb_ref[...]
    f = lb + (1.0 - lb) * jax.nn.sigmoid(z_ref[...])
    kk = 1.0 - f
    lf = jnp.log(f)
    r_i = lax.broadcasted_iota(jnp.int32, (L, L), 0)
    c_i = lax.broadcasted_iota(jnp.int32, (L, L), 1)
    tri = (c_i >= r_i) if reverse else (c_i <= r_i)
    cum = _cumsum_rows(tri.astype(BF16), lf)
    last_row = 0 if reverse else L - 1
    last = cum[last_row:last_row + 1]
    q = q_ref[...]
    v = v_ref[...]
    qe = (q * jnp.exp(cum)).astype(BF16)
    kdec = (kk * jnp.exp(last - cum)).astype(BF16)
    e_last = jnp.exp(last)
    v_bf = v.astype(BF16)
    ones = jnp.ones((C_HD, C_HD), BF16)
    sub_r = lax.broadcasted_iota(jnp.int32, (SB, C_HD), 0)
    order = list(range(nsb - 1, -1, -1)) if reverse else list(range(nsb))

    outs = []
    for h in range(C_HEADS):
        hs = slice(h * C_HD, (h + 1) * C_HD)
        st = st_ref[h]
        o_h = _dot_nt(qe[:, hs], st.astype(BF16))
        st_ref[h] = st * e_last[:, hs] + _dot_tn(v_bf[:, hs], kdec[:, hs])
        cum_h, q_h, k_h, v_h = cum[:, hs], q[:, hs], kk[:, hs], v[:, hs]
        o_sub = [None] * nsb
        for p, bi in enumerate(order):
            rows = slice(bi * SB, (bi + 1) * SB)
            cum_i, q_i, k_i, v_i = cum_h[rows], q_h[rows], k_h[rows], v_h[rows]
            acc = jnp.zeros((SB, C_HD), F32)
            w_rows = []
            for s in range(SB):
                ok = (sub_r <= s) if reverse else (sub_r >= s)
                e = jnp.where(ok, jnp.exp(cum_i - cum_i[s:s + 1]), 0.0)
                w_rows.append((q_i * e * k_i[s:s + 1]).astype(BF16))
            red = _dot(jnp.concatenate(w_rows, axis=0), ones)
            for s in range(SB):
                acc = acc + red[s * SB:(s + 1) * SB] * v_i[s:s + 1]
            if p > 0:
                prev = order[p - 1]
                b_row = prev * SB if reverse else prev * SB + SB - 1
                b = cum_h[b_row:b_row + 1]
                if reverse:
                    past = slice((bi + 1) * SB, L)
                else:
                    past = slice(0, bi * SB)
                qi = (q_i * jnp.exp(cum_i - b)).astype(BF16)
                kp = (k_h[past] * jnp.exp(b - cum_h[past])).astype(BF16)
                att = _dot_nt(qi, kp)
                acc = acc + _dot(att.astype(BF16), v_bf[past, hs])
            o_sub[bi] = acc
        outs.append(o_h + jnp.concatenate(o_sub, axis=0))
    o_ref[...] = jnp.concatenate(outs, axis=1)


def _scan(p5, lb, zcol, reverse, n_lat, n_ctx, batch):
    t_all = p5.shape[0]
    L = C_CHUNK
    w = C_HEADS * C_HD
    nl, nc = n_lat // L, n_ctx // L
    steps = nl + nc

    def row_idx(b, c):
        if reverse:
            ctx = batch * nl + b * nc + (nc - 1 - c)
            lat = b * nl + (nl - 1 - (c - nc))
        else:
            ctx = batch * nl + b * nc + c
            lat = b * nl + (c - nc)
        return jnp.where(c < nc, ctx, lat)

    col = lambda cb: pl.BlockSpec((L, w), lambda b, c: (row_idx(b, c), cb))
    return pl.pallas_call(
        functools.partial(_scan_kernel, reverse=reverse),
        grid=(batch, steps),
        in_specs=[col(0), col(zcol), col(3), pl.BlockSpec((1, w), lambda b, c: (0, 0))],
        out_specs=pl.BlockSpec((L, w), lambda b, c: (row_idx(b, c), 0)),
        out_shape=jax.ShapeDtypeStruct((t_all, w), F32),
        scratch_shapes=[pltpu.VMEM((C_HEADS, C_HD, C_HD), F32)],
        compiler_params=pltpu.CompilerParams(dimension_semantics=("arbitrary", "arbitrary")),
        name="hgrn_scan_rev" if reverse else "hgrn_scan_fwd",
    )(p5, p5, p5, lb)


def _moe_kernel(be_ref, nu_ref, x_ref, w1g_ref, w1l_ref, b1g_ref, b1l_ref, w2_ref, b2_ref, gate_ref, o_ref):
    i = pl.program_id(0)

    @pl.when(i < nu_ref[0])
    def _():
        x = x_ref[...]
        hg = _dot(x, w1g_ref[...]) + b1g_ref[...]
        hl = _dot(x, w1l_ref[...]) + b1l_ref[...]
        glu = jnp.minimum(hg, SWIGLU_LIMIT)
        lin = jnp.clip(hl, -SWIGLU_LIMIT, SWIGLU_LIMIT)
        y = glu * jax.nn.sigmoid(SWIGLU_ALPHA * glu) * (lin + 1.0)
        o_ref[...] = (_dot(y.astype(BF16), w2_ref[...]) + b2_ref[...]) * gate_ref[...]

    @pl.when(i >= nu_ref[0])
    def _():
        o_ref[...] = jnp.zeros(o_ref.shape, F32)


def _moe_experts(xb, block_expert, n_used, w1g, w1l, b1g, b1l, w2, b2, row_gate):
    n_rows, d = xb.shape
    tm = MOE_TILE
    dff = w2.shape[1]
    wspec = lambda k, n: pl.BlockSpec((None, k, n), lambda i, be, nu: (be[i], 0, 0))
    grid_spec = pltpu.PrefetchScalarGridSpec(
        num_scalar_prefetch=2,
        grid=(n_rows // tm,),
        in_specs=[
            pl.BlockSpec((tm, d), lambda i, be, nu: (i, 0)),
            wspec(d, dff), wspec(d, dff), wspec(1, dff), wspec(1, dff), wspec(dff, d), wspec(1, d),
            pl.BlockSpec((tm, 1), lambda i, be, nu: (i, 0)),
        ],
        out_specs=pl.BlockSpec((tm, d), lambda i, be, nu: (i, 0)),
    )
    return pl.pallas_call(
        _moe_kernel,
        grid_spec=grid_spec,
        out_shape=jax.ShapeDtypeStruct((n_rows, d), F32),
        name="moe_experts",
    )(block_expert, n_used, xb, w1g, w1l, b1g, b1l, w2, b2, row_gate)


def _moe(v_bf, logits, w1g, w1l, b1g, b1l, w2, b2):
    t, d = v_bf.shape
    tm = MOE_TILE
    top_val, top_idx = lax.top_k(logits[:, :N_EXPERTS], TOP_K)
    gates = jax.nn.softmax(top_val, axis=-1)
    n_assign = t * TOP_K
    flat_e = top_idx.reshape(-1).astype(jnp.int32)
    flat_tok = jnp.arange(n_assign, dtype=jnp.int32) // TOP_K
    order = jnp.argsort(flat_e)
    e_sorted = flat_e[order]
    counts = jnp.bincount(flat_e, length=N_EXPERTS).astype(jnp.int32)
    group_start = jnp.cumsum(counts) - counts
    padded = (counts + tm - 1) // tm * tm
    pad_end = jnp.cumsum(padded)
    pad_start = pad_end - padded
    dest = pad_start[e_sorted] + (jnp.arange(n_assign, dtype=jnp.int32) - group_start[e_sorted])
    n_blocks = n_assign // tm + N_EXPERTS
    n_rows = n_blocks * tm
    row_tok = jnp.full((n_rows,), t, jnp.int32).at[dest].set(flat_tok[order])
    row_gate = jnp.zeros((n_rows,), F32).at[dest].set(gates.reshape(-1)[order])
    block_start = jnp.arange(n_blocks, dtype=jnp.int32) * tm
    block_expert = jnp.minimum(jnp.searchsorted(pad_end, block_start, side="right"), N_EXPERTS - 1).astype(jnp.int32)
    n_used = (pad_end[-1] // tm).astype(jnp.int32).reshape(1)
    v_pad = jnp.concatenate([v_bf, jnp.zeros((1, d), v_bf.dtype)], axis=0)
    y = _moe_experts(v_pad[row_tok], block_expert, n_used, w1g, w1l, b1g, b1l, w2, b2, row_gate.reshape(n_rows, 1))
    return jnp.zeros((t + 1, d), F32).at[row_tok].add(y)[:t]


def _final_kernel(h_ref, g_ref, o_ref):
    o_ref[...] = _rms(h_ref[...], g_ref[...])


def _final_norm(h_lat, g):
    t, d = h_lat.shape
    tm = ROW_TILE
    return pl.pallas_call(
        _final_kernel,
        grid=(t // tm,),
        in_specs=[pl.BlockSpec((tm, d), lambda i: (i, 0)), pl.BlockSpec((1, d), lambda i: (0, 0))],
        out_specs=pl.BlockSpec((tm, d), lambda i: (i, 0)),
        out_shape=jax.ShapeDtypeStruct((t, d), F32),
        name="final_norm",
    )(h_lat, g.reshape(1, d))


def _rope_tables(n_lat, n_ctx_pad):
    quarter = A_HD // 4
    inv_freq = ROPE_THETA ** (-jnp.arange(quarter, dtype=F32) / quarter)
    t = jnp.arange(n_lat)
    row = (t // GRID_W).astype(F32)
    colp = (t % GRID_W).astype(F32)
    ang_r = row[:, None] * inv_freq
    ang_c = colp[:, None] * inv_freq
    cos64 = jnp.concatenate([jnp.cos(ang_r), jnp.cos(ang_r), jnp.cos(ang_c), jnp.cos(ang_c)], axis=1)
    sin64 = jnp.concatenate([-jnp.sin(ang_r), jnp.sin(ang_r), -jnp.sin(ang_c), jnp.sin(ang_c)], axis=1)
    one, zero = jnp.ones_like(cos64), jnp.zeros_like(sin64)

    def pad(a, fill):
        return jnp.concatenate([a, jnp.full((n_ctx_pad, a.shape[1]), fill, F32)], axis=0)

    cos_a = pad(jnp.concatenate([cos64, cos64], axis=1), 1.0)
    sin_a = pad(jnp.concatenate([sin64, sin64], axis=1), 0.0)
    cos_b = pad(jnp.concatenate([cos64, one], axis=1), 1.0)
    sin_b = pad(jnp.concatenate([sin64, zero], axis=1), 0.0)
    return cos_a, sin_a, cos_b, sin_b


def _prep_uq(w_uq):
    r = w_uq.shape[0]
    w = w_uq.reshape(r, B_HEADS, B_NOPE + B_ROPE)
    w = jnp.pad(w, ((0, 0), (0, 0), (0, 2 * LANES - B_NOPE - B_ROPE)))
    return w.reshape(r, B_HEADS * 2 * LANES).astype(BF16)


def _prep_ukv(w_ukv):
    r = w_ukv.shape[0]
    w = w_ukv.reshape(r, B_HEADS, B_NOPE + B_VD)
    wk = w[:, :, :B_NOPE].reshape(r, B_HEADS * B_NOPE)
    wv = w[:, :, B_NOPE:].reshape(r, B_HEADS * B_VD)
    return wk.astype(BF16), wv.astype(BF16)


def kernel(x, c, ctx, c_ctx, norm_mix_g, norm_ffn_g, w_ada, b_ada, w_in_ab, diff_lambda, diff_subln_g, mla_q_norm_g, mla_kv_norm_g, w_uq, w_ukv, w_out_ab, w_in_c, lb_raw, hgrn_norm_g, w_out_c, w_router, b_router, w_exp1, b_exp1, w_exp2, b_exp2, final_g):
    batch, n_lat, d = x.shape
    n_ctx = ctx.shape[1]
    t_lat = batch * n_lat
    h_all = jnp.concatenate([x.reshape(t_lat, d), ctx.reshape(batch * n_ctx, d)], axis=0)

    cvec = jnp.concatenate([c, c_ctx[None, :], jnp.zeros((MOD_ROWS - batch - 1, d), F32)], axis=0)
    mods = _ada(cvec, w_ada, b_ada)

    lb_p = jax.nn.softmax(lb_raw.astype(F32), axis=0)
    lower_bounds = jnp.cumsum(lb_p, axis=0) - lb_p[0]
    tabs = _rope_tables(n_lat, PREP_TILE)

    kv_len = n_lat + n_ctx
    tq = min(1024, n_lat)
    tk = next(t for t in (768, 512, 256) if kv_len % t == 0)

    for l in range(DEPTH):
        j = l // 2
        m = mods[l]
        sh1, sc1, g1, sh2, sc2, g2 = [m[:, k * d:(k + 1) * d].reshape(MOD_ROWS, 1, d) for k in range(6)]
        wr = jnp.pad(w_router[l], ((0, 0), (0, LANES - N_EXPERTS)))
        wrh, wrl = _split_hi_lo(wr)
        br = jnp.pad(b_router[l].astype(F32), (0, LANES - N_EXPERTS)).reshape(1, LANES)
        tail = (h_all, g1, norm_ffn_g[l], sc2, sh2, wrh, wrl, br, n_lat)
        row = lambda w: pl.BlockSpec((ROW_TILE, w), lambda i: (i, 0))
        full = lambda a: pl.BlockSpec(a.shape, lambda i: (0,) * a.ndim)
        if l % 2 == 0:
            lam_init = 0.8 - 0.6 * math.exp(-0.3 * l)
            w_in = jnp.pad(w_in_ab[j], ((0, 0), (0, AB_PROJ_PAD - w_in_ab.shape[2]))).astype(BF16)
            p = _inproj(h_all, norm_mix_g[l], sc1, sh1, w_in, n_lat, AB_PROJ_PAD)
            wuk, wuv = _prep_ukv(w_ukv[j])
            qa, qb, ka, va, kb, vb = _abprep(
                p, tabs, mla_q_norm_g[j].reshape(1, -1), mla_kv_norm_g[j].reshape(1, -1), _prep_uq(w_uq[j]), wuk, wuv,
                n_lat, n_ctx, batch)
            extra = (diff_lambda[j].astype(F32), diff_subln_g[j].reshape(1, A_VD))
            lat = dict(batch=batch, tq=tq, tk=tk, q_blk0=0, q_blk_stride=n_lat // tq, nq=n_lat // tq,
                       kv_blk_stride=kv_len // tk, nkv=kv_len // tk)
            oa = _flash("diff", qa, ka, va, extra, heads=A_HEADS, lam_init=lam_init, **lat)
            ob = _flash("mla", qb, kb, vb, (), heads=B_HEADS, **lat)
            if l != DEPTH - 1:
                cq = dict(batch=batch, tq=n_ctx, tk=n_ctx, q_blk0=t_lat // n_ctx, q_blk_stride=1, nq=1,
                          kv_blk_stride=kv_len // n_ctx, nkv=1)
                oac = _flash("diff", qa, ka, va, extra, heads=A_HEADS, lam_init=lam_init, **cq)
                obc = _flash("mla", qb, kb, vb, (), heads=B_HEADS, **cq)
                oa = jnp.concatenate([oa[:t_lat], oac[t_lat:]], axis=0)
                ob = jnp.concatenate([ob[:t_lat], obc[t_lat:]], axis=0)
            wo = w_out_ab[j].astype(BF16)
            wa, wb = wo[:A_HEADS * A_VD], wo[A_HEADS * A_VD:]
            h_mid, v_bf, logits = _outproj(
                "ab", (oa, ob, wa, wb), [row(oa.shape[1]), row(ob.shape[1]), full(wa), full(wb)], *tail)
        else:
            p5 = _inproj(h_all, norm_mix_g[l], sc1, sh1, w_in_c[j].astype(BF16), n_lat, 1024)
            lb = lower_bounds[l].reshape(1, -1)
            o_f = _scan(p5, lb, 1, False, n_lat, n_ctx, batch)
            o_b = _scan(p5, lb, 2, True, n_lat, n_ctx, batch)
            ng = hgrn_norm_g[j].reshape(1, C_HD)
            wo = w_out_c[j].astype(BF16)
            gate_spec = pl.BlockSpec((ROW_TILE, d), lambda i: (i, 4))
            h_mid, v_bf, logits = _outproj(
                "c", (o_f, o_b, p5, ng, wo), [row(d), row(d), gate_spec, full(ng), full(wo)], *tail)

        w1 = w_exp1[l]
        w1g, w1l = w1[:, :, 0::2].astype(BF16), w1[:, :, 1::2].astype(BF16)
        b1 = b_exp1[l].astype(F32)
        b1g, b1l = b1[:, None, 0::2], b1[:, None, 1::2]
        moe = _moe(v_bf, logits, w1g, w1l, b1g, b1l, w_exp2[l].astype(BF16), b_exp2[l].astype(F32)[:, None, :])
        g2_rows = jnp.concatenate(
            [jnp.repeat(g2[:batch, 0], n_lat, axis=0), jnp.broadcast_to(g2[batch, 0], (batch * n_ctx, d))], axis=0)
        h_all = h_mid + g2_rows * moe

    return _final_norm(h_all[:t_lat], final_g).reshape(batch, n_lat, d)
```

```python
import functools
import math

import jax
import jax.numpy as jnp
from jax import lax
from jax.experimental import pallas as pl
from jax.experimental.pallas import tpu as pltpu

F32 = jnp.float32
BF16 = jnp.bfloat16

D_MODEL = 1024
DEPTH = 4
GRID_W = 64
ROPE_THETA = 10000.0
NORM_EPS = 1e-6

A_HEADS = 4
A_HD = 64
A_VD = 128
A_SCALE = A_HD ** -0.5
B_HEADS = 4
B_NOPE = 128
B_ROPE = 64
B_VD = 128
B_Q_RANK = 256
B_KV_RANK = 128
B_SCALE = (B_NOPE + B_ROPE) ** -0.5
AB_PROJ_PAD = 2048

C_HEADS = 8
C_HD = 128
C_CHUNK = 64
C_SUB = 16

N_EXPERTS = 32
TOP_K = 4
SWIGLU_ALPHA = 1.702
SWIGLU_LIMIT = 7.0

LANES = 128
SUBLANES = 8
ROW_TILE = 512
PREP_TILE = 256
MOE_TILE = 256
COMBINE_TILE = 256
MOD_ROWS = 8
MOE_VMEM_BYTES = 52 * 1024 * 1024


def _split_hi_lo(x):
    hi = x.astype(BF16)
    lo = (x - hi.astype(F32)).astype(BF16)
    return hi, lo


def _dot(a, b):
    return jnp.dot(a, b, preferred_element_type=F32)


def _dot_nt(a, b):
    return lax.dot_general(a, b, (((1,), (1,)), ((), ())), preferred_element_type=F32)


def _dot_tn(a, b):
    return lax.dot_general(a, b, (((0,), (0,)), ((), ())), preferred_element_type=F32)


def _rms(x, g):
    ms = jnp.mean(x * x, axis=-1, keepdims=True)
    return x * lax.rsqrt(ms + NORM_EPS) * g


def _ada_kernel(c_ref, w_ref, b_ref, o_ref):
    c = c_ref[...]
    s = c * jax.nn.sigmoid(c)
    s_hi, s_lo = _split_hi_lo(s)
    w_hi, w_lo = _split_hi_lo(w_ref[...])
    o_ref[...] = _dot(s_hi, w_hi) + _dot(s_lo, w_hi) + _dot(s_hi, w_lo) + b_ref[...]


def _ada(cvec, w_ada, b_ada):
    depth, d, n6 = w_ada.shape
    tn = 1536
    return pl.pallas_call(
        _ada_kernel,
        grid=(depth, n6 // tn),
        in_specs=[
            pl.BlockSpec((MOD_ROWS, d), lambda l, j: (0, 0)),
            pl.BlockSpec((None, d, tn), lambda l, j: (l, 0, j)),
            pl.BlockSpec((None, 1, tn), lambda l, j: (l, 0, j)),
        ],
        out_specs=pl.BlockSpec((None, MOD_ROWS, tn), lambda l, j: (l, 0, j)),
        out_shape=jax.ShapeDtypeStruct((depth, MOD_ROWS, n6), F32),
        name="ada_mod",
    )(cvec, w_ada, b_ada.reshape(depth, 1, n6))


def _row_copy(src, s, dst, d, sem):
    s8 = pl.multiple_of(s * SUBLANES, SUBLANES)
    d8 = pl.multiple_of(d * SUBLANES, SUBLANES)
    return pltpu.make_async_copy(src.at[pl.ds(s8, SUBLANES), :], dst.at[pl.ds(d8, SUBLANES), :], sem)


def _unpack_rows(ref, tm):
    return jnp.concatenate([ref[pl.ds(s, tm, stride=SUBLANES), :] for s in range(SUBLANES)], axis=1)


def _pack_rows(ref, x):
    tm = x.shape[0]
    for s in range(SUBLANES):
        ref[pl.ds(s, tm, stride=SUBLANES), :] = x[:, s * LANES:(s + 1) * LANES]


def _moe_combine(i, dest_hbm, y_hbm, gate_ref, idx_smem, ybuf, sems):
    tm = ybuf.shape[1] // SUBLANES
    ids = pltpu.make_async_copy(dest_hbm.at[i], idx_smem, sems.at[0])
    ids.start()
    ids.wait()

    def issue(r, carry):
        for k in range(TOP_K):
            _row_copy(y_hbm, idx_smem[r * TOP_K + k], ybuf.at[k], r, sems.at[1]).start()
        return carry

    lax.fori_loop(0, tm, issue, 0, unroll=4)

    def drain(r, carry):
        _row_copy(y_hbm, 0, ybuf.at[0], 0, sems.at[1]).wait()
        return carry

    lax.fori_loop(0, tm * TOP_K, drain, 0, unroll=8)
    gates = gate_ref[...]
    acc = gates[:, 0:1] * _unpack_rows(ybuf.at[0], tm)
    for k in range(1, TOP_K):
        acc = acc + gates[:, k:k + 1] * _unpack_rows(ybuf.at[k], tm)
    return acc


def _inproj_kernel(h_ref, g_ref, sc_ref, sh_ref, w_ref, o_ref, u_scr):
    @pl.when(pl.program_id(1) == 0)
    def _():
        u = _rms(h_ref[...], g_ref[...]) * (1.0 + sc_ref[...]) + sh_ref[...]
        u_scr[...] = u.astype(BF16)

    o_ref[...] = _dot(u_scr[...], w_ref[...])


def _inproj_moe_kernel(h_ref, g2_ref, gate_ref, dest_hbm, y_hbm, g_ref, sc_ref, sh_ref, w_ref, o_ref, hn_ref,
                       u_scr, idx_smem, ybuf, sems):
    @pl.when(pl.program_id(1) == 0)
    def _():
        moe = _moe_combine(pl.program_id(0), dest_hbm, y_hbm, gate_ref, idx_smem, ybuf, sems)
        h = h_ref[...] + g2_ref[...] * moe
        hn_ref[...] = h
        u = _rms(h, g_ref[...]) * (1.0 + sc_ref[...]) + sh_ref[...]
        u_scr[...] = u.astype(BF16)

    o_ref[...] = _dot(u_scr[...], w_ref[...])


def _mod_spec(n_lat_blocks):
    def idx(i, *_):
        return (jnp.minimum(i // n_lat_blocks, 2), 0, 0)

    return pl.BlockSpec((None, 1, D_MODEL), idx)


def _combine_scratch(tm, d):
    assert d == SUBLANES * LANES
    return [pltpu.SMEM((tm * TOP_K,), jnp.int32), pltpu.VMEM((TOP_K, tm * SUBLANES, LANES), F32),
            pltpu.SemaphoreType.DMA((2,))]


def _inproj(h_all, gain, sc, sh, w_bf, n_lat, tn, pending=None):
    t_all, d = h_all.shape
    nout = w_bf.shape[1]
    tm = COMBINE_TILE
    mod = _mod_spec(n_lat // tm)
    row = pl.BlockSpec((tm, d), lambda i, j: (i, 0))
    tail_specs = [pl.BlockSpec((1, d), lambda i, j: (0, 0)), mod, mod, pl.BlockSpec((d, tn), lambda i, j: (0, j))]
    tail = (gain.reshape(1, d), sc, sh, w_bf)
    p_spec = pl.BlockSpec((tm, tn), lambda i, j: (i, j))
    p_shape = jax.ShapeDtypeStruct((t_all, nout), F32)
    params = pltpu.CompilerParams(dimension_semantics=("arbitrary", "arbitrary"))
    if pending is None:
        p = pl.pallas_call(
            _inproj_kernel,
            grid=(t_all // tm, nout // tn),
            in_specs=[row] + tail_specs,
            out_specs=p_spec,
            out_shape=p_shape,
            scratch_shapes=[pltpu.VMEM((tm, d), BF16)],
            compiler_params=params,
            name="inproj",
        )(h_all, *tail)
        return p, h_all
    g2, gates, dest, y_sorted = pending
    any_spec = pl.BlockSpec(memory_space=pl.ANY)
    return pl.pallas_call(
        _inproj_moe_kernel,
        grid=(t_all // tm, nout // tn),
        in_specs=[row, mod, pl.BlockSpec((tm, LANES), lambda i, j: (i, 0)), any_spec, any_spec] + tail_specs,
        out_specs=[p_spec, row],
        out_shape=[p_shape, jax.ShapeDtypeStruct((t_all, d), F32)],
        scratch_shapes=[pltpu.VMEM((tm, d), BF16)] + _combine_scratch(tm, d),
        compiler_params=params,
        name="inproj_moe",
    )(h_all, g2, gates, dest, y_sorted, *tail)


def _rope(x, cos, sin):
    n = x.shape[-1]
    lane = lax.broadcasted_iota(jnp.int32, x.shape, 1)
    first = (lane // 16) % 2 == 0
    partner = jnp.where(first, pltpu.roll(x, n - 16, 1), pltpu.roll(x, 16, 1))
    return x * cos + partner * sin


def _abprep_kernel(p_ref, cosa_ref, sina_ref, cosb_ref, sinb_ref, qg_ref, kvg_ref, wuq_ref, wuk_ref, wuv_ref,
                   qa_ref, qb_ref, ka_ref, va_ref, kb_ref, vb_ref):
    cosa, sina = cosa_ref[...], sina_ref[...]
    cosb, sinb = cosb_ref[...], sinb_ref[...]
    n_a = 2 * A_HEADS * A_HD
    qa = [_rope(p_ref[:, c:c + LANES], cosa, sina) * A_SCALE for c in range(0, n_a, LANES)]
    qa_ref[...] = jnp.concatenate(qa, axis=1).astype(BF16)
    ka = [_rope(p_ref[:, n_a + c:n_a + c + LANES], cosa, sina) for c in range(0, n_a, LANES)]
    ka_ref[...] = jnp.concatenate(ka, axis=1).astype(BF16)
    va_ref[...] = p_ref[:, 2 * n_a:2 * n_a + A_HEADS * A_VD].astype(BF16)

    off = 2 * n_a + A_HEADS * A_VD
    cq = _rms(p_ref[:, off:off + B_Q_RANK], qg_ref[...]).astype(BF16)
    qf = _dot(cq, wuq_ref[...])
    ckv = _rms(p_ref[:, off + B_Q_RANK:off + B_Q_RANK + B_KV_RANK], kvg_ref[...]).astype(BF16)
    kn = _dot(ckv, wuk_ref[...])
    vb_ref[...] = _dot(ckv, wuv_ref[...]).astype(BF16)
    kr_off = off + B_Q_RANK + B_KV_RANK
    krr = _rope(p_ref[:, kr_off:kr_off + LANES], cosb, sinb)
    qb, kb = [], []
    for h in range(B_HEADS):
        qb.append(qf[:, 2 * LANES * h:2 * LANES * h + LANES] * B_SCALE)
        qb.append(_rope(qf[:, 2 * LANES * h + LANES:2 * LANES * (h + 1)], cosb, sinb) * B_SCALE)
        kb.append(kn[:, LANES * h:LANES * (h + 1)])
        kb.append(krr)
    qb_ref[...] = jnp.concatenate(qb, axis=1).astype(BF16)
    kb_ref[...] = jnp.concatenate(kb, axis=1).astype(BF16)


def _abprep(p, tabs, qg, kvg, wuq, wuk, wuv, n_lat, n_ctx, batch):
    t_all = p.shape[0]
    tm = PREP_TILE
    nlb = n_lat // tm
    ncb = n_ctx // tm
    kvb = nlb + ncb

    def tab_idx(i):
        return (jnp.where(i < batch * nlb, i % nlb, nlb), 0)

    def kv_idx(i):
        lat = (i // nlb) * kvb + ncb + i % nlb
        j = i - batch * nlb
        ctx = (j // ncb) * kvb + j % ncb
        return (jnp.where(i < batch * nlb, lat, ctx), 0)

    tab_spec = pl.BlockSpec((tm, LANES), tab_idx)
    full = lambda a: pl.BlockSpec(a.shape, lambda i: (0,) * a.ndim)
    wq, wk, wv = 2 * A_HEADS * A_HD, B_HEADS * 2 * LANES, A_HEADS * A_VD
    kv_rows = batch * (n_lat + n_ctx)
    return pl.pallas_call(
        _abprep_kernel,
        grid=(t_all // tm,),
        in_specs=[pl.BlockSpec((tm, AB_PROJ_PAD), lambda i: (i, 0)), tab_spec, tab_spec, tab_spec, tab_spec,
                  full(qg), full(kvg), full(wuq), full(wuk), full(wuv)],
        out_specs=[
            pl.BlockSpec((tm, wq), lambda i: (i, 0)),
            pl.BlockSpec((tm, wk), lambda i: (i, 0)),
            pl.BlockSpec((tm, wq), kv_idx),
            pl.BlockSpec((tm, wv), kv_idx),
            pl.BlockSpec((tm, wk), kv_idx),
            pl.BlockSpec((tm, wv), kv_idx),
        ],
        out_shape=[
            jax.ShapeDtypeStruct((t_all, wq), BF16),
            jax.ShapeDtypeStruct((t_all, wk), BF16),
            jax.ShapeDtypeStruct((kv_rows, wq), BF16),
            jax.ShapeDtypeStruct((kv_rows, wv), BF16),
            jax.ShapeDtypeStruct((kv_rows, wk), BF16),
            jax.ShapeDtypeStruct((kv_rows, wv), BF16),
        ],
        name="ab_prep",
    )(p, *tabs, qg, kvg, wuq, wuk, wuv)


def _softmax_step(s, v, m_ref, l_ref, acc_ref, idx):
    m_prev = m_ref[idx]
    m_new = jnp.maximum(m_prev, jnp.max(s, axis=-1, keepdims=True))
    alpha = jnp.exp(m_prev - m_new)
    p = jnp.exp(s - m_new)
    l_ref[idx] = alpha * l_ref[idx] + jnp.sum(p, axis=-1, keepdims=True)
    acc_ref[idx] = alpha * acc_ref[idx] + _dot(p.astype(BF16), v)
    m_ref[idx] = m_new


def _flash_init(m_ref, l_ref, acc_ref):
    m_ref[...] = jnp.full(m_ref.shape, -jnp.inf, F32)
    l_ref[...] = jnp.zeros(l_ref.shape, F32)
    acc_ref[...] = jnp.zeros(acc_ref.shape, F32)


def _flash_diff_kernel(q_ref, k_ref, v_ref, lam_ref, g_ref, o_ref, m_ref, l_ref, acc_ref, *, lam_init):
    j = pl.program_id(3)

    @pl.when(j == 0)
    def _():
        _flash_init(m_ref, l_ref, acc_ref)

    q = q_ref[...]
    k = k_ref[...]
    v = v_ref[...]
    lane = lax.broadcasted_iota(jnp.int32, q.shape, 1)
    zero = jnp.zeros_like(q)
    _softmax_step(_dot_nt(jnp.where(lane < A_HD, q, zero), k), v, m_ref, l_ref, acc_ref, 0)
    _softmax_step(_dot_nt(jnp.where(lane >= A_HD, q, zero), k), v, m_ref, l_ref, acc_ref, 1)

    @pl.when(j == pl.num_programs(3) - 1)
    def _():
        lf = lam_ref[...]
        lam = (jnp.exp(jnp.sum(lf[0:1] * lf[1:2], axis=-1, keepdims=True))
               - jnp.exp(jnp.sum(lf[2:3] * lf[3:4], axis=-1, keepdims=True)) + lam_init)
        o = acc_ref[0] / l_ref[0] - lam * (acc_ref[1] / l_ref[1])
        o_ref[...] = (_rms(o, g_ref[...]) * (1.0 - lam_init)).astype(o_ref.dtype)


def _flash_mla_kernel(q_ref, k_ref, v_ref, o_ref, m_ref, l_ref, acc_ref):
    j = pl.program_id(3)

    @pl.when(j == 0)
    def _():
        _flash_init(m_ref, l_ref, acc_ref)

    _softmax_step(_dot_nt(q_ref[...], k_ref[...]), v_ref[...], m_ref, l_ref, acc_ref, 0)

    @pl.when(j == pl.num_programs(3) - 1)
    def _():
        o_ref[...] = (acc_ref[0] / l_ref[0]).astype(o_ref.dtype)


def _flash(kind, q, k, v, extra, *, batch, heads, tq, tk, q_blk0, q_blk_stride, nq, kv_blk_stride, nkv, lam_init=0.0,
           out_init=None):
    wq = q.shape[1] // heads
    wv = v.shape[1] // heads
    n_maps = 2 if kind == "diff" else 1
    q_spec = pl.BlockSpec((tq, wq), lambda b, h, i, j: (q_blk0 + b * q_blk_stride + i, h))
    k_spec = pl.BlockSpec((tk, wq), lambda b, h, i, j: (b * kv_blk_stride + j, h))
    v_spec = pl.BlockSpec((tk, wv), lambda b, h, i, j: (b * kv_blk_stride + j, h))
    o_spec = pl.BlockSpec((tq, wv), lambda b, h, i, j: (q_blk0 + b * q_blk_stride + i, h))
    in_specs = [q_spec, k_spec, v_spec]
    if kind == "diff":
        body = functools.partial(_flash_diff_kernel, lam_init=lam_init)
        in_specs += [pl.BlockSpec(e.shape, lambda b, h, i, j: (0, 0)) for e in extra]
    else:
        body = _flash_mla_kernel
    args = [q, k, v, *extra]
    aliases = {}
    if out_init is not None:
        n_in = len(args)
        in_specs.append(pl.BlockSpec(memory_space=pl.ANY))
        args.append(out_init)
        aliases = {n_in: 0}
        inner = body
        body = lambda *refs: inner(*refs[:n_in], *refs[n_in + 1:])
    return pl.pallas_call(
        body,
        grid=(batch, heads, nq, nkv),
        in_specs=in_specs,
        out_specs=o_spec,
        input_output_aliases=aliases,
        out_shape=jax.ShapeDtypeStruct((q.shape[0], heads * wv), BF16),
        scratch_shapes=[
            pltpu.VMEM((n_maps, tq, 1), F32),
            pltpu.VMEM((n_maps, tq, 1), F32),
            pltpu.VMEM((n_maps, tq, wv), F32),
        ],
        compiler_params=pltpu.CompilerParams(
            dimension_semantics=("arbitrary", "arbitrary", "arbitrary", "arbitrary")),
        name="flash_" + kind,
    )(*args)


def _out_epilogue(y, h_ref, g1_ref, nf_ref, sc2_ref, sh2_ref, wrh_ref, wrl_ref, br_ref, hn_ref, v_ref, lg_ref):
    hn = h_ref[...] + g1_ref[...] * y
    hn_ref[...] = hn
    v = _rms(hn, nf_ref[...]) * (1.0 + sc2_ref[...]) + sh2_ref[...]
    _pack_rows(v_ref, v)
    v_hi, v_lo = _split_hi_lo(v)
    wrh = wrh_ref[...]
    lg_ref[...] = _dot(v_hi, wrh) + _dot(v_lo, wrh) + _dot(v_hi, wrl_ref[...]) + br_ref[...]


def _about_kernel(oa_ref, ob_ref, wa_ref, wb_ref, *rest):
    y = _dot(oa_ref[...], wa_ref[...]) + _dot(ob_ref[...], wb_ref[...])
    _out_epilogue(y, *rest)


def _cout_kernel(of_ref, obk_ref, gate_ref, ng_ref, w_ref, *rest):
    o = of_ref[...] + obk_ref[...]
    ng = ng_ref[...]
    parts = [_rms(o[:, c:c + C_HD], ng) for c in range(0, C_HEADS * C_HD, C_HD)]
    g = gate_ref[...]
    x = jnp.concatenate(parts, axis=1) * (g * jax.nn.sigmoid(g))
    _out_epilogue(_dot(x.astype(BF16), w_ref[...]), *rest)


def _outproj(kind, ins, in_specs, h_all, g1, nf, sc2, sh2, wrh, wrl, br, n_lat):
    t_all, d = h_all.shape
    tm = ROW_TILE
    mod = _mod_spec(n_lat // tm)
    row = lambda w: pl.BlockSpec((tm, w), lambda i: (i, 0))
    full = lambda a: pl.BlockSpec(a.shape, lambda i: (0,) * a.ndim)
    nf = nf.reshape(1, d)
    return pl.pallas_call(
        _about_kernel if kind == "ab" else _cout_kernel,
        grid=(t_all // tm,),
        in_specs=in_specs + [row(d), mod, full(nf), mod, mod, full(wrh), full(wrl), full(br)],
        out_specs=[row(d), pl.BlockSpec((tm * SUBLANES, LANES), lambda i: (i, 0)), row(LANES)],
        out_shape=[
            jax.ShapeDtypeStruct((t_all, d), F32),
            jax.ShapeDtypeStruct((t_all * SUBLANES, LANES), F32),
            jax.ShapeDtypeStruct((t_all, LANES), F32),
        ],
        name="outproj_" + kind,
    )(*ins, h_all, g1, nf, sc2, sh2, wrh, wrl, br)


def _cumsum_rows(tri_bf, x):
    hi = x.astype(BF16)
    r1 = x - hi.astype(F32)
    mid = r1.astype(BF16)
    lo = (r1 - mid.astype(F32)).astype(BF16)
    return _dot(tri_bf, hi) + _dot(tri_bf, mid) + _dot(tri_bf, lo)


def _scan_kernel(q_ref, z_ref, v_ref, lb_ref, o_ref, st_ref, *, reverse):
    c = pl.program_id(1)

    @pl.when(c == 0)
    def _():
        st_ref[...] = jnp.zeros(st_ref.shape, F32)

    L, SB = C_CHUNK, C_SUB
    nsb = L // SB
    lb = lb_ref[...]
    f = lb + (1.0 - lb) * jax.nn.sigmoid(z_ref[...])
    kk = 1.0 - f
    lf = jnp.log(f)
    r_i = lax.broadcasted_iota(jnp.int32, (L, L), 0)
    c_i = lax.broadcasted_iota(jnp.int32, (L, L), 1)
    tri = (c_i >= r_i) if reverse else (c_i <= r_i)
    cum = _cumsum_rows(tri.astype(BF16), lf)
    last_row = 0 if reverse else L - 1
    last = cum[last_row:last_row + 1]
    q = q_ref[...]
    v = v_ref[...]
    qe = (q * jnp.exp(cum)).astype(BF16)
    kdec = (kk * jnp.exp(last - cum)).astype(BF16)
    e_last = jnp.exp(last)
    v_bf = v.astype(BF16)
    ones = jnp.ones((C_HD, C_HD), BF16)
    sub_r = lax.broadcasted_iota(jnp.int32, (SB, C_HD), 0)
    order = list(range(nsb - 1, -1, -1)) if reverse else list(range(nsb))

    outs = []
    for h in range(C_HEADS):
        hs = slice(h * C_HD, (h + 1) * C_HD)
        st = st_ref[h]
        o_h = _dot_nt(qe[:, hs], st.astype(BF16))
        st_ref[h] = st * e_last[:, hs] + _dot_tn(v_bf[:, hs], kdec[:, hs])
        cum_h, q_h, k_h, v_h = cum[:, hs], q[:, hs], kk[:, hs], v[:, hs]
        o_sub = [None] * nsb
        for p, bi in enumerate(order):
            rows = slice(bi * SB, (bi + 1) * SB)
            cum_i, q_i, k_i, v_i = cum_h[rows], q_h[rows], k_h[rows], v_h[rows]
            acc = jnp.zeros((SB, C_HD), F32)
            w_rows = []
            for s in range(SB):
                ok = (sub_r <= s) if reverse else (sub_r >= s)
                e = jnp.where(ok, jnp.exp(cum_i - cum_i[s:s + 1]), 0.0)
                w_rows.append((q_i * e * k_i[s:s + 1]).astype(BF16))
            red = _dot(jnp.concatenate(w_rows, axis=0), ones)
            for s in range(SB):
                acc = acc + red[s * SB:(s + 1) * SB] * v_i[s:s + 1]
            if p > 0:
                prev = order[p - 1]
                b_row = prev * SB if reverse else prev * SB + SB - 1
                b = cum_h[b_row:b_row + 1]
                if reverse:
                    past = slice((bi + 1) * SB, L)
                else:
                    past = slice(0, bi * SB)
                qi = (q_i * jnp.exp(cum_i - b)).astype(BF16)
                kp = (k_h[past] * jnp.exp(b - cum_h[past])).astype(BF16)
                att = _dot_nt(qi, kp)
                acc = acc + _dot(att.astype(BF16), v_bf[past, hs])
            o_sub[bi] = acc
        outs.append(o_h + jnp.concatenate(o_sub, axis=0))
    o_ref[...] = jnp.concatenate(outs, axis=1)


def _scan(p5, lb, zcol, reverse, n_lat, n_ctx, batch):
    t_all = p5.shape[0]
    L = C_CHUNK
    w = C_HEADS * C_HD
    nl, nc = n_lat // L, n_ctx // L
    steps = nl + nc

    def row_idx(b, c):
        if reverse:
            ctx = batch * nl + b * nc + (nc - 1 - c)
            lat = b * nl + (nl - 1 - (c - nc))
        else:
            ctx = batch * nl + b * nc + c
            lat = b * nl + (c - nc)
        return jnp.where(c < nc, ctx, lat)

    col = lambda cb: pl.BlockSpec((L, w), lambda b, c: (row_idx(b, c), cb))
    return pl.pallas_call(
        functools.partial(_scan_kernel, reverse=reverse),
        grid=(batch, steps),
        in_specs=[col(0), col(zcol), col(3), pl.BlockSpec((1, w), lambda b, c: (0, 0))],
        out_specs=pl.BlockSpec((L, w), lambda b, c: (row_idx(b, c), 0)),
        out_shape=jax.ShapeDtypeStruct((t_all, w), F32),
        scratch_shapes=[pltpu.VMEM((C_HEADS, C_HD, C_HD), F32)],
        compiler_params=pltpu.CompilerParams(dimension_semantics=("arbitrary", "arbitrary")),
        name="hgrn_scan_rev" if reverse else "hgrn_scan_fwd",
    )(p5, p5, p5, lb)


def _route_kernel(lg_ref, tri_ref, sel_ref, gate_ref, cnt_ref, carry_ref):
    i = pl.program_id(0)

    @pl.when(i == 0)
    def _():
        carry_ref[...] = jnp.zeros(carry_ref.shape, F32)

    lg = lg_ref[...]
    lane = lax.broadcasted_iota(jnp.int32, lg.shape, 1)
    lane_f = lane.astype(F32)
    x = jnp.where(lane < N_EXPERTS, lg, -jnp.inf)
    vals, idxs, hits = [], [], []
    for _ in range(TOP_K):
        mk = jnp.max(x, axis=-1, keepdims=True)
        ik = jnp.min(jnp.where(x == mk, lane_f, float(LANES)), axis=-1, keepdims=True)
        hit = lane_f == ik
        x = jnp.where(hit, -jnp.inf, x)
        vals.append(mk)
        idxs.append(ik.astype(jnp.int32))
        hits.append(hit)
    member = hits[0] | hits[1] | hits[2] | hits[3]
    member_f = jnp.where(member, 1.0, 0.0)
    before = _dot(tri_ref[...], member_f.astype(BF16)) + carry_ref[0:1]
    carry_ref[0:1] = carry_ref[0:1] + jnp.sum(member_f, axis=0, keepdims=True)
    exps = [jnp.exp(v - vals[0]) for v in vals]
    denom = exps[0] + exps[1] + exps[2] + exps[3]
    sel = jnp.zeros(lg.shape, jnp.int32)
    gate = jnp.zeros(lg.shape, F32)
    for k in range(TOP_K):
        rank = jnp.sum(jnp.where(hits[k], before, 0.0), axis=-1, keepdims=True).astype(jnp.int32)
        sel = jnp.where(lane == k, idxs[k], sel)
        sel = jnp.where(lane == TOP_K + k, rank, sel)
        gate = jnp.where(lane == k, exps[k] / denom, gate)
    sel_ref[...] = sel
    gate_ref[...] = gate
    cnt_ref[...] = carry_ref[...]


def _route(logits):
    t = logits.shape[0]
    tm = ROW_TILE
    tri = (jnp.arange(tm)[:, None] > jnp.arange(tm)[None, :]).astype(BF16)
    blk = pl.BlockSpec((tm, LANES), lambda i: (i, 0))
    return pl.pallas_call(
        _route_kernel,
        grid=(t // tm,),
        in_specs=[blk, pl.BlockSpec((tm, tm), lambda i: (0, 0))],
        out_specs=[blk, blk, pl.BlockSpec((MOD_ROWS, LANES), lambda i: (0, 0))],
        out_shape=[
            jax.ShapeDtypeStruct((t, LANES), jnp.int32),
            jax.ShapeDtypeStruct((t, LANES), F32),
            jax.ShapeDtypeStruct((MOD_ROWS, LANES), F32),
        ],
        scratch_shapes=[pltpu.VMEM((MOD_ROWS, LANES), F32)],
        compiler_params=pltpu.CompilerParams(dimension_semantics=("arbitrary",)),
        name="route",
    )(logits, tri)


def _dispatch_kernel(zs_ref, v_ref, dest_hbm, x_hbm, idx_smem, zbuf, sems):
    i = pl.program_id(0)
    tm = v_ref.shape[0] // SUBLANES
    pad = zbuf.shape[0]

    @pl.when(i == 0)
    def _():
        zbuf[...] = jnp.zeros(zbuf.shape, F32)
        for e in range(N_EXPERTS):
            z0 = pl.multiple_of(zs_ref[e] * SUBLANES, SUBLANES)
            pltpu.make_async_copy(zbuf, x_hbm.at[pl.ds(z0, pad), :], sems.at[2]).start()
        for e in range(N_EXPERTS):
            pltpu.make_async_copy(zbuf, x_hbm.at[pl.ds(0, pad), :], sems.at[2]).wait()

    ids = pltpu.make_async_copy(dest_hbm.at[i], idx_smem, sems.at[0])
    ids.start()
    ids.wait()

    def issue(r, carry):
        for k in range(TOP_K):
            _row_copy(v_ref, r, x_hbm, idx_smem[r * TOP_K + k], sems.at[1]).start()
        return carry

    lax.fori_loop(0, tm, issue, 0, unroll=4)

    def drain(r, carry):
        _row_copy(v_ref, 0, x_hbm, 0, sems.at[1]).wait()
        return carry

    lax.fori_loop(0, tm * TOP_K, drain, 0, unroll=8)


def _dispatch(v, dest, zero_start, n_rows):
    t = v.shape[0] // SUBLANES
    tm = COMBINE_TILE
    grid_spec = pltpu.PrefetchScalarGridSpec(
        num_scalar_prefetch=1,
        grid=(t // tm,),
        in_specs=[pl.BlockSpec((tm * SUBLANES, LANES), lambda i, zs: (i, 0)), pl.BlockSpec(memory_space=pl.ANY)],
        out_specs=pl.BlockSpec(memory_space=pl.ANY),
        scratch_shapes=[pltpu.SMEM((tm * TOP_K,), jnp.int32), pltpu.VMEM((MOE_TILE * SUBLANES, LANES), F32),
                        pltpu.SemaphoreType.DMA((3,))],
    )
    return pl.pallas_call(
        _dispatch_kernel,
        grid_spec=grid_spec,
        out_shape=jax.ShapeDtypeStruct(((n_rows + MOE_TILE) * SUBLANES, LANES), F32),
        compiler_params=pltpu.CompilerParams(dimension_semantics=("arbitrary",)),
        name="dispatch",
    )(zero_start, v, dest)


def _moe_kernel(be_ref, nu_ref, first_ref, x_ref, w1_ref, b1g_ref, b1l_ref, w2_ref, b2_ref, perm_ref, o_ref,
                w1g_scr, w1l_scr, w2_scr):
    i = pl.program_id(0)
    live = i < nu_ref[0]

    @pl.when(jnp.logical_and(live, first_ref[i] == 1))
    def _():
        perm = perm_ref[...]
        group = 2 * LANES
        for b in range(w1_ref.shape[1] // group):
            r = _dot(w1_ref[:, b * group:(b + 1) * group].astype(BF16), perm)
            w1g_scr[:, b * LANES:(b + 1) * LANES] = r[:, :LANES].astype(BF16)
            w1l_scr[:, b * LANES:(b + 1) * LANES] = r[:, LANES:].astype(BF16)
        w2_scr[...] = w2_ref[...].astype(BF16)

    @pl.when(live)
    def _():
        x = _unpack_rows(x_ref, MOE_TILE).astype(BF16)
        hg = _dot(x, w1g_scr[...]) + b1g_ref[...]
        hl = _dot(x, w1l_scr[...]) + b1l_ref[...]
        glu = jnp.minimum(hg, SWIGLU_LIMIT)
        lin = jnp.clip(hl, -SWIGLU_LIMIT, SWIGLU_LIMIT)
        y = glu * jax.nn.sigmoid(SWIGLU_ALPHA * glu) * (lin + 1.0)
        _pack_rows(o_ref, _dot(y.astype(BF16), w2_scr[...]) + b2_ref[...])

    @pl.when(jnp.logical_not(live))
    def _():
        o_ref[...] = jnp.zeros(o_ref.shape, F32)


def _deinterleave_perm():
    src = jnp.arange(2 * LANES)
    dst = jnp.where(src % 2 == 0, src // 2, LANES + src // 2)
    return (dst[:, None] == jnp.arange(2 * LANES)[None, :]).astype(BF16)


def _moe_experts(xb, n_rows, block_expert, n_used, first, layer, w1, b1g, b1l, w2, b2):
    d = w1.shape[2]
    tm = MOE_TILE
    dff = w2.shape[2]
    perm = _deinterleave_perm()
    tile = pl.BlockSpec((tm * SUBLANES, LANES), lambda i, be, nu, fi: (i, 0))
    wspec = lambda k, n: pl.BlockSpec((None, k, n), lambda i, be, nu, fi: (be[i], 0, 0))
    lwspec = lambda k, n: pl.BlockSpec((None, None, k, n), lambda i, be, nu, fi: (layer, be[i], 0, 0))
    grid_spec = pltpu.PrefetchScalarGridSpec(
        num_scalar_prefetch=3,
        grid=(n_rows // tm,),
        in_specs=[
            tile,
            lwspec(d, 2 * dff), wspec(1, dff), wspec(1, dff), lwspec(dff, d), wspec(1, d),
            pl.BlockSpec(perm.shape, lambda i, be, nu, fi: (0, 0)),
        ],
        out_specs=tile,
        scratch_shapes=[pltpu.VMEM((d, dff), BF16), pltpu.VMEM((d, dff), BF16), pltpu.VMEM((dff, d), BF16)],
    )
    return pl.pallas_call(
        _moe_kernel,
        grid_spec=grid_spec,
        out_shape=jax.ShapeDtypeStruct((n_rows * SUBLANES, LANES), F32),
        compiler_params=pltpu.CompilerParams(dimension_semantics=("arbitrary",), vmem_limit_bytes=MOE_VMEM_BYTES),
        name="moe_experts",
    )(block_expert, n_used, first, xb, w1, b1g, b1l, w2, b2, perm)


def _moe(v, logits, layer, w1, b1g, b1l, w2, b2):
    t = logits.shape[0]
    tm = MOE_TILE
    sel, gates, cnt = _route(logits)
    counts = cnt[0, :N_EXPERTS].astype(jnp.int32)
    padded = (counts + tm - 1) // tm * tm
    pad_end = jnp.cumsum(padded)
    pad_start = pad_end - padded
    n_blocks = t * TOP_K // tm + N_EXPERTS
    n_rows = n_blocks * tm
    block_start = jnp.arange(n_blocks, dtype=jnp.int32) * tm
    block_expert = jnp.minimum(jnp.sum(block_start[:, None] >= pad_end[None, :], axis=1), N_EXPERTS - 1)
    block_expert = block_expert.astype(jnp.int32)
    n_used = (pad_end[-1] // tm).astype(jnp.int32).reshape(1)
    first = jnp.concatenate([jnp.ones((1,), jnp.int32), (block_expert[1:] != block_expert[:-1]).astype(jnp.int32)])
    expert = sel[:, :TOP_K]
    base = jnp.sum(jnp.where(expert[:, :, None] == jnp.arange(N_EXPERTS)[None, None, :], pad_start[None, None, :], 0),
                   axis=-1)
    dest = (base + sel[:, TOP_K:2 * TOP_K]).astype(jnp.int32).reshape(t // COMBINE_TILE, COMBINE_TILE * TOP_K)
    x_sorted = _dispatch(v, dest, (pad_start + counts).astype(jnp.int32), n_rows)
    y_sorted = _moe_experts(x_sorted, n_rows, block_expert, n_used, first, layer, w1, b1g, b1l, w2, b2)
    return gates, dest, y_sorted


def _final_kernel(h_ref, g2_ref, gate_ref, dest_hbm, y_hbm, g_ref, o_ref, idx_smem, ybuf, sems):
    moe = _moe_combine(pl.program_id(0), dest_hbm, y_hbm, gate_ref, idx_smem, ybuf, sems)
    o_ref[...] = _rms(h_ref[...] + g2_ref[...] * moe, g_ref[...])


def _final_norm(h_all, pending, g, t_lat, n_lat):
    d = h_all.shape[1]
    tm = COMBINE_TILE
    g2, gates, dest, y_sorted = pending
    row = pl.BlockSpec((tm, d), lambda i: (i, 0))
    any_spec = pl.BlockSpec(memory_space=pl.ANY)
    return pl.pallas_call(
        _final_kernel,
        grid=(t_lat // tm,),
        in_specs=[row, _mod_spec(n_lat // tm), pl.BlockSpec((tm, LANES), lambda i: (i, 0)), any_spec, any_spec,
                  pl.BlockSpec((1, d), lambda i: (0, 0))],
        out_specs=row,
        out_shape=jax.ShapeDtypeStruct((t_lat, d), F32),
        scratch_shapes=_combine_scratch(tm, d),
        compiler_params=pltpu.CompilerParams(dimension_semantics=("arbitrary",)),
        name="final_norm",
    )(h_all, g2, gates, dest, y_sorted, g.reshape(1, d))


def _rope_tables(n_lat, n_ctx_pad):
    quarter = A_HD // 4
    inv_freq = ROPE_THETA ** (-jnp.arange(quarter, dtype=F32) / quarter)
    t = jnp.arange(n_lat)
    row = (t // GRID_W).astype(F32)
    colp = (t % GRID_W).astype(F32)
    ang_r = row[:, None] * inv_freq
    ang_c = colp[:, None] * inv_freq
    cos64 = jnp.concatenate([jnp.cos(ang_r), jnp.cos(ang_r), jnp.cos(ang_c), jnp.cos(ang_c)], axis=1)
    sin64 = jnp.concatenate([-jnp.sin(ang_r), jnp.sin(ang_r), -jnp.sin(ang_c), jnp.sin(ang_c)], axis=1)
    one, zero = jnp.ones_like(cos64), jnp.zeros_like(sin64)

    def pad(a, fill):
        return jnp.concatenate([a, jnp.full((n_ctx_pad, a.shape[1]), fill, F32)], axis=0)

    cos_a = pad(jnp.concatenate([cos64, cos64], axis=1), 1.0)
    sin_a = pad(jnp.concatenate([sin64, sin64], axis=1), 0.0)
    cos_b = pad(jnp.concatenate([cos64, one], axis=1), 1.0)
    sin_b = pad(jnp.concatenate([sin64, zero], axis=1), 0.0)
    return cos_a, sin_a, cos_b, sin_b


def _prep_uq(w_uq):
    r = w_uq.shape[0]
    w = w_uq.reshape(r, B_HEADS, B_NOPE + B_ROPE)
    w = jnp.pad(w, ((0, 0), (0, 0), (0, 2 * LANES - B_NOPE - B_ROPE)))
    return w.reshape(r, B_HEADS * 2 * LANES).astype(BF16)


def _prep_ukv(w_ukv):
    r = w_ukv.shape[0]
    w = w_ukv.reshape(r, B_HEADS, B_NOPE + B_VD)
    wk = w[:, :, :B_NOPE].reshape(r, B_HEADS * B_NOPE)
    wv = w[:, :, B_NOPE:].reshape(r, B_HEADS * B_VD)
    return wk.astype(BF16), wv.astype(BF16)


def kernel(x, c, ctx, c_ctx, norm_mix_g, norm_ffn_g, w_ada, b_ada, w_in_ab, diff_lambda, diff_subln_g, mla_q_norm_g, mla_kv_norm_g, w_uq, w_ukv, w_out_ab, w_in_c, lb_raw, hgrn_norm_g, w_out_c, w_router, b_router, w_exp1, b_exp1, w_exp2, b_exp2, final_g):
    batch, n_lat, d = x.shape
    n_ctx = ctx.shape[1]
    t_lat = batch * n_lat
    h_all = jnp.concatenate([x.reshape(t_lat, d), ctx.reshape(batch * n_ctx, d)], axis=0)

    cvec = jnp.concatenate([c, c_ctx[None, :], jnp.zeros((MOD_ROWS - batch - 1, d), F32)], axis=0)
    mods = _ada(cvec, w_ada, b_ada)

    lb_p = jax.nn.softmax(lb_raw.astype(F32), axis=0)
    lower_bounds = jnp.cumsum(lb_p, axis=0) - lb_p[0]
    tabs = _rope_tables(n_lat, PREP_TILE)

    kv_len = n_lat + n_ctx
    tq = min(1024, n_lat)
    tk = next(t for t in (768, 512, 256) if kv_len % t == 0)

    pending = None
    for l in range(DEPTH):
        j = l // 2
        m = mods[l]
        sh1, sc1, g1, sh2, sc2, g2 = [m[:, k * d:(k + 1) * d].reshape(MOD_ROWS, 1, d) for k in range(6)]
        wr = jnp.pad(w_router[l], ((0, 0), (0, LANES - N_EXPERTS)))
        wrh, wrl = _split_hi_lo(wr)
        br = jnp.pad(b_router[l].astype(F32), (0, LANES - N_EXPERTS)).reshape(1, LANES)
        row = lambda w: pl.BlockSpec((ROW_TILE, w), lambda i: (i, 0))
        full = lambda a: pl.BlockSpec(a.shape, lambda i: (0,) * a.ndim)
        if l % 2 == 0:
            w_in, tn = jnp.pad(w_in_ab[j], ((0, 0), (0, AB_PROJ_PAD - w_in_ab.shape[2]))).astype(BF16), AB_PROJ_PAD
        else:
            w_in, tn = w_in_c[j].astype(BF16), 1024
        p, h_all = _inproj(h_all, norm_mix_g[l], sc1, sh1, w_in, n_lat, tn, pending)
        tail = (h_all, g1, norm_ffn_g[l], sc2, sh2, wrh, wrl, br, n_lat)
        if l % 2 == 0:
            lam_init = 0.8 - 0.6 * math.exp(-0.3 * l)
            wuk, wuv = _prep_ukv(w_ukv[j])
            qa, qb, ka, va, kb, vb = _abprep(
                p, tabs, mla_q_norm_g[j].reshape(1, -1), mla_kv_norm_g[j].reshape(1, -1), _prep_uq(w_uq[j]), wuk, wuv,
                n_lat, n_ctx, batch)
            extra = (diff_lambda[j].astype(F32), diff_subln_g[j].reshape(1, A_VD))
            lat = dict(batch=batch, tq=tq, tk=tk, q_blk0=0, q_blk_stride=n_lat // tq, nq=n_lat // tq,
                       kv_blk_stride=kv_len // tk, nkv=kv_len // tk)
            oa = _flash("diff", qa, ka, va, extra, heads=A_HEADS, lam_init=lam_init, **lat)
            ob = _flash("mla", qb, kb, vb, (), heads=B_HEADS, **lat)
            if l != DEPTH - 1:
                cq = dict(batch=batch, tq=n_ctx, tk=n_ctx, q_blk0=t_lat // n_ctx, q_blk_stride=1, nq=1,
                          kv_blk_stride=kv_len // n_ctx, nkv=1)
                oa = _flash("diff", qa, ka, va, extra, heads=A_HEADS, lam_init=lam_init, out_init=oa, **cq)
                ob = _flash("mla", qb, kb, vb, (), heads=B_HEADS, out_init=ob, **cq)
            wo = w_out_ab[j].astype(BF16)
            wa, wb = wo[:A_HEADS * A_VD], wo[A_HEADS * A_VD:]
            h_mid, v_ffn, logits = _outproj(
                "ab", (oa, ob, wa, wb), [row(oa.shape[1]), row(ob.shape[1]), full(wa), full(wb)], *tail)
        else:
            p5 = p
            lb = lower_bounds[l].reshape(1, -1)
            o_f = _scan(p5, lb, 1, False, n_lat, n_ctx, batch)
            o_b = _scan(p5, lb, 2, True, n_lat, n_ctx, batch)
            ng = hgrn_norm_g[j].reshape(1, C_HD)
            wo = w_out_c[j].astype(BF16)
            gate_spec = pl.BlockSpec((ROW_TILE, d), lambda i: (i, 4))
            h_mid, v_ffn, logits = _outproj(
                "c", (o_f, o_b, p5, ng, wo), [row(d), row(d), gate_spec, full(ng), full(wo)], *tail)

        b1 = b_exp1[l].astype(F32)
        b1g, b1l = b1[:, None, 0::2], b1[:, None, 1::2]
        pending = (g2,) + _moe(v_ffn, logits, l, w_exp1, b1g, b1l, w_exp2, b_exp2[l].astype(F32)[:, None, :])
        h_all = h_mid

    return _final_norm(h_all, pending, final_g, t_lat, n_lat).reshape(batch, n_lat, d)
```

```python
import functools
import math

import jax
import jax.numpy as jnp
from jax import lax
from jax.experimental import pallas as pl
from jax.experimental.pallas import tpu as pltpu

F32 = jnp.float32
BF16 = jnp.bfloat16

D_MODEL = 1024
DEPTH = 4
GRID_W = 64
ROPE_THETA = 10000.0
NORM_EPS = 1e-6

A_HEADS = 4
A_HD = 64
A_VD = 128
A_SCALE = A_HD ** -0.5
B_HEADS = 4
B_NOPE = 128
B_ROPE = 64
B_VD = 128
B_Q_RANK = 256
B_KV_RANK = 128
B_SCALE = (B_NOPE + B_ROPE) ** -0.5
LOG2E = math.log2(math.e)
AB_PROJ_PAD = 2048

C_HEADS = 8
C_HD = 128
C_CHUNK = 64
C_SUB = 16

N_EXPERTS = 32
TOP_K = 4
SWIGLU_ALPHA = 1.702
SWIGLU_LIMIT = 7.0

LANES = 128
SUBLANES = 8
ROW_TILE = 512
PREP_TILE = 256
FLASH_SUB = 256
MOE_TILE = 256
COMBINE_TILE = 256
MOD_ROWS = 8
MOE_VMEM_BYTES = 52 * 1024 * 1024


def _split_hi_lo(x):
    hi = x.astype(BF16)
    lo = (x - hi.astype(F32)).astype(BF16)
    return hi, lo


def _dot(a, b):
    return jnp.dot(a, b, preferred_element_type=F32)


def _dot_nt(a, b):
    return lax.dot_general(a, b, (((1,), (1,)), ((), ())), preferred_element_type=F32)


def _dot_tn(a, b):
    return lax.dot_general(a, b, (((0,), (0,)), ((), ())), preferred_element_type=F32)


def _rms(x, g):
    ms = jnp.mean(x * x, axis=-1, keepdims=True)
    return x * lax.rsqrt(ms + NORM_EPS) * g


def _ada_kernel(c_ref, w_ref, b_ref, o_ref):
    c = c_ref[...]
    s = c * jax.nn.sigmoid(c)
    s_hi, s_lo = _split_hi_lo(s)
    w_hi, w_lo = _split_hi_lo(w_ref[...])
    o_ref[...] = _dot(s_hi, w_hi) + _dot(s_lo, w_hi) + _dot(s_hi, w_lo) + b_ref[...]


def _ada(cvec, w_ada, b_ada):
    depth, d, n6 = w_ada.shape
    tn = 1536
    return pl.pallas_call(
        _ada_kernel,
        grid=(depth, n6 // tn),
        in_specs=[
            pl.BlockSpec((MOD_ROWS, d), lambda l, j: (0, 0)),
            pl.BlockSpec((None, d, tn), lambda l, j: (l, 0, j)),
            pl.BlockSpec((None, 1, tn), lambda l, j: (l, 0, j)),
        ],
        out_specs=pl.BlockSpec((None, MOD_ROWS, tn), lambda l, j: (l, 0, j)),
        out_shape=jax.ShapeDtypeStruct((depth, MOD_ROWS, n6), F32),
        name="ada_mod",
    )(cvec, w_ada, b_ada.reshape(depth, 1, n6))


def _row_copy(src, s, dst, d, sem):
    s8 = pl.multiple_of(s * SUBLANES, SUBLANES)
    d8 = pl.multiple_of(d * SUBLANES, SUBLANES)
    return pltpu.make_async_copy(src.at[pl.ds(s8, SUBLANES), :], dst.at[pl.ds(d8, SUBLANES), :], sem)


def _unpack_rows(ref, tm):
    return jnp.concatenate([ref[pl.ds(s, tm, stride=SUBLANES), :] for s in range(SUBLANES)], axis=1)


def _pack_rows(ref, x):
    tm = x.shape[0]
    for s in range(SUBLANES):
        ref[pl.ds(s, tm, stride=SUBLANES), :] = x[:, s * LANES:(s + 1) * LANES]


def _moe_combine(i, dest_hbm, y_hbm, gate_ref, idx_smem, ybuf, sems):
    tm = ybuf.shape[1] // SUBLANES
    ids = pltpu.make_async_copy(dest_hbm.at[i], idx_smem, sems.at[0])
    ids.start()
    ids.wait()

    def issue(r, carry):
        for k in range(TOP_K):
            _row_copy(y_hbm, idx_smem[r * TOP_K + k], ybuf.at[k], r, sems.at[1]).start(priority=k % 2)
        return carry

    lax.fori_loop(0, tm, issue, 0, unroll=4)

    def drain(r, carry):
        _row_copy(y_hbm, 0, ybuf.at[0], 0, sems.at[1]).wait()
        return carry

    lax.fori_loop(0, tm * TOP_K, drain, 0, unroll=8)
    gates = gate_ref[...]
    acc = gates[:, 0:1] * _unpack_rows(ybuf.at[0], tm)
    for k in range(1, TOP_K):
        acc = acc + gates[:, k:k + 1] * _unpack_rows(ybuf.at[k], tm)
    return acc


def _inproj_kernel(h_ref, g_ref, sc_ref, sh_ref, w_ref, o_ref, u_scr):
    @pl.when(pl.program_id(1) == 0)
    def _():
        u = _rms(h_ref[...], g_ref[...]) * (1.0 + sc_ref[...]) + sh_ref[...]
        u_scr[...] = u.astype(BF16)

    o_ref[...] = _dot(u_scr[...], w_ref[...])


def _inproj_moe_kernel(h_ref, g2_ref, gate_ref, dest_hbm, y_hbm, g_ref, sc_ref, sh_ref, w_ref, o_ref, hn_ref,
                       u_scr, idx_smem, ybuf, sems):
    @pl.when(pl.program_id(1) == 0)
    def _():
        moe = _moe_combine(pl.program_id(0), dest_hbm, y_hbm, gate_ref, idx_smem, ybuf, sems)
        h = h_ref[...] + g2_ref[...] * moe
        hn_ref[...] = h
        u = _rms(h, g_ref[...]) * (1.0 + sc_ref[...]) + sh_ref[...]
        u_scr[...] = u.astype(BF16)

    o_ref[...] = _dot(u_scr[...], w_ref[...])


def _mod_spec(n_lat_blocks):
    def idx(i, *_):
        return (jnp.minimum(i // n_lat_blocks, 2), 0, 0)

    return pl.BlockSpec((None, 1, D_MODEL), idx)


def _combine_scratch(tm, d):
    assert d == SUBLANES * LANES
    return [pltpu.SMEM((tm * TOP_K,), jnp.int32), pltpu.VMEM((TOP_K, tm * SUBLANES, LANES), F32),
            pltpu.SemaphoreType.DMA((2,))]


def _inproj(h_all, gain, sc, sh, w_bf, n_lat, tn, pending=None):
    t_all, d = h_all.shape
    nout = w_bf.shape[1]
    tm = COMBINE_TILE
    mod = _mod_spec(n_lat // tm)
    row = pl.BlockSpec((tm, d), lambda i, j: (i, 0))
    tail_specs = [pl.BlockSpec((1, d), lambda i, j: (0, 0)), mod, mod, pl.BlockSpec((d, tn), lambda i, j: (0, j))]
    tail = (gain.reshape(1, d), sc, sh, w_bf)
    p_spec = pl.BlockSpec((tm, tn), lambda i, j: (i, j))
    p_shape = jax.ShapeDtypeStruct((t_all, nout), F32)
    params = pltpu.CompilerParams(dimension_semantics=("arbitrary", "arbitrary"))
    if pending is None:
        p = pl.pallas_call(
            _inproj_kernel,
            grid=(t_all // tm, nout // tn),
            in_specs=[row] + tail_specs,
            out_specs=p_spec,
            out_shape=p_shape,
            scratch_shapes=[pltpu.VMEM((tm, d), BF16)],
            compiler_params=params,
            name="inproj",
        )(h_all, *tail)
        return p, h_all
    g2, gates, dest, y_sorted = pending
    any_spec = pl.BlockSpec(memory_space=pl.ANY)
    return pl.pallas_call(
        _inproj_moe_kernel,
        grid=(t_all // tm, nout // tn),
        in_specs=[row, mod, pl.BlockSpec((tm, LANES), lambda i, j: (i, 0)), any_spec, any_spec] + tail_specs,
        out_specs=[p_spec, row],
        out_shape=[p_shape, jax.ShapeDtypeStruct((t_all, d), F32)],
        scratch_shapes=[pltpu.VMEM((tm, d), BF16)] + _combine_scratch(tm, d),
        compiler_params=params,
        name="inproj_moe",
    )(h_all, g2, gates, dest, y_sorted, *tail)


def _rope(x, cos, sin):
    n = x.shape[-1]
    lane = lax.broadcasted_iota(jnp.int32, x.shape, 1)
    first = (lane // 16) % 2 == 0
    partner = jnp.where(first, pltpu.roll(x, n - 16, 1), pltpu.roll(x, 16, 1))
    return x * cos + partner * sin


def _abprep_kernel(p_ref, cosa_ref, sina_ref, cosb_ref, sinb_ref, qg_ref, kvg_ref, wuq_ref, wuk_ref, wuv_ref,
                   qa_ref, qb_ref, ka_ref, va_ref, kb_ref, vb_ref):
    cosa, sina = cosa_ref[...], sina_ref[...]
    cosb, sinb = cosb_ref[...], sinb_ref[...]
    n_a = 2 * A_HEADS * A_HD
    qa = [_rope(p_ref[:, c:c + LANES], cosa, sina) * (A_SCALE * LOG2E) for c in range(0, n_a, LANES)]
    qa_ref[...] = jnp.concatenate(qa, axis=1).astype(BF16)
    ka = [_rope(p_ref[:, n_a + c:n_a + c + LANES], cosa, sina) for c in range(0, n_a, LANES)]
    ka_ref[...] = jnp.concatenate(ka, axis=1).astype(BF16)
    va_ref[...] = p_ref[:, 2 * n_a:2 * n_a + A_HEADS * A_VD].astype(BF16)

    off = 2 * n_a + A_HEADS * A_VD
    cq = _rms(p_ref[:, off:off + B_Q_RANK], qg_ref[...]).astype(BF16)
    qf = _dot(cq, wuq_ref[...])
    ckv = _rms(p_ref[:, off + B_Q_RANK:off + B_Q_RANK + B_KV_RANK], kvg_ref[...]).astype(BF16)
    kn = _dot(ckv, wuk_ref[...])
    vb_ref[...] = _dot(ckv, wuv_ref[...]).astype(BF16)
    kr_off = off + B_Q_RANK + B_KV_RANK
    krr = _rope(p_ref[:, kr_off:kr_off + LANES], cosb, sinb)
    qb, kb = [], []
    for h in range(B_HEADS):
        qb.append(qf[:, 2 * LANES * h:2 * LANES * h + LANES] * (B_SCALE * LOG2E))
        qb.append(_rope(qf[:, 2 * LANES * h + LANES:2 * LANES * (h + 1)], cosb, sinb) * (B_SCALE * LOG2E))
        kb.append(kn[:, LANES * h:LANES * (h + 1)])
        kb.append(krr)
    qb_ref[...] = jnp.concatenate(qb, axis=1).astype(BF16)
    kb_ref[...] = jnp.concatenate(kb, axis=1).astype(BF16)


def _abprep(p, tabs, qg, kvg, wuq, wuk, wuv, n_lat, n_ctx, batch):
    t_all = p.shape[0]
    tm = PREP_TILE
    nlb = n_lat // tm
    ncb = n_ctx // tm
    kvb = nlb + ncb

    def tab_idx(i):
        return (jnp.where(i < batch * nlb, i % nlb, nlb), 0)

    def kv_idx(i):
        lat = (i // nlb) * kvb + ncb + i % nlb
        j = i - batch * nlb
        ctx = (j // ncb) * kvb + j % ncb
        return (jnp.where(i < batch * nlb, lat, ctx), 0)

    tab_spec = pl.BlockSpec((tm, LANES), tab_idx)
    full = lambda a: pl.BlockSpec(a.shape, lambda i: (0,) * a.ndim)
    wq, wk, wv = 2 * A_HEADS * A_HD, B_HEADS * 2 * LANES, A_HEADS * A_VD
    kv_rows = batch * (n_lat + n_ctx)
    return pl.pallas_call(
        _abprep_kernel,
        grid=(t_all // tm,),
        in_specs=[pl.BlockSpec((tm, AB_PROJ_PAD), lambda i: (i, 0)), tab_spec, tab_spec, tab_spec, tab_spec,
                  full(qg), full(kvg), full(wuq), full(wuk), full(wuv)],
        out_specs=[
            pl.BlockSpec((tm, wq), lambda i: (i, 0)),
            pl.BlockSpec((tm, wk), lambda i: (i, 0)),
            pl.BlockSpec((tm, wq), kv_idx),
            pl.BlockSpec((tm, wv), kv_idx),
            pl.BlockSpec((tm, wk), kv_idx),
            pl.BlockSpec((tm, wv), kv_idx),
        ],
        out_shape=[
            jax.ShapeDtypeStruct((t_all, wq), BF16),
            jax.ShapeDtypeStruct((t_all, wk), BF16),
            jax.ShapeDtypeStruct((kv_rows, wq), BF16),
            jax.ShapeDtypeStruct((kv_rows, wv), BF16),
            jax.ShapeDtypeStruct((kv_rows, wk), BF16),
            jax.ShapeDtypeStruct((kv_rows, wv), BF16),
        ],
        name="ab_prep",
    )(p, *tabs, qg, kvg, wuq, wuk, wuv)


def _flash_steps(j, q_maps, k_ref, v_ref, m_refs, acc_refs):
    @pl.when(j == 0)
    def _():
        for m_ref, acc_ref in zip(m_refs, acc_refs):
            m_ref[...] = jnp.full(m_ref.shape, -jnp.inf, F32)
            acc_ref[...] = jnp.zeros(acc_ref.shape, F32)

    k = k_ref[...]
    v = v_ref[...]
    v_ext = jnp.concatenate([v, jnp.ones(v.shape, BF16)], axis=1)
    tq = m_refs[0].shape[0]
    sub = min(FLASH_SUB, tq)
    for r0 in range(0, tq, sub):
        rows = slice(r0, r0 + sub)
        for q_of, m_ref, acc_ref in zip(q_maps, m_refs, acc_refs):
            s = _dot_nt(q_of(rows), k)
            m_prev = m_ref[rows, :]
            m_new = jnp.maximum(m_prev, jnp.max(s, axis=-1, keepdims=True))
            p = jnp.exp2(s - m_new).astype(BF16)
            acc_ref[rows, :] = jnp.exp2(m_prev - m_new) * acc_ref[rows, :] + _dot(p, v_ext)
            m_ref[rows, :] = m_new


def _flash_result(acc_ref):
    wv = acc_ref.shape[1] // 2
    return acc_ref[:, :wv] / acc_ref[:, wv:]


def _flash_diff_kernel(q_ref, k_ref, v_ref, lam_ref, g_ref, o_ref, m1_ref, m2_ref, acc1_ref, acc2_ref, *, lam_init):
    j = pl.program_id(3)

    def q_half(upper):
        def get(rows):
            q = q_ref[rows, :]
            lane = lax.broadcasted_iota(jnp.int32, q.shape, 1)
            return jnp.where((lane >= A_HD) == upper, q, jnp.zeros_like(q))
        return get

    _flash_steps(j, (q_half(False), q_half(True)), k_ref, v_ref, (m1_ref, m2_ref), (acc1_ref, acc2_ref))

    @pl.when(j == pl.num_programs(3) - 1)
    def _():
        lf = lam_ref[...]
        lam = (jnp.exp(jnp.sum(lf[0:1] * lf[1:2], axis=-1, keepdims=True))
               - jnp.exp(jnp.sum(lf[2:3] * lf[3:4], axis=-1, keepdims=True)) + lam_init)
        o = _flash_result(acc1_ref) - lam * _flash_result(acc2_ref)
        o_ref[...] = (_rms(o, g_ref[...]) * (1.0 - lam_init)).astype(o_ref.dtype)


def _flash_mla_kernel(q_ref, k_ref, v_ref, o_ref, m_ref, acc_ref):
    j = pl.program_id(3)
    _flash_steps(j, (lambda rows: q_ref[rows, :],), k_ref, v_ref, (m_ref,), (acc_ref,))

    @pl.when(j == pl.num_programs(3) - 1)
    def _():
        o_ref[...] = _flash_result(acc_ref).astype(o_ref.dtype)


def _flash(kind, q, k, v, extra, *, batch, heads, tq, tk, q_blk0, q_blk_stride, nq, kv_blk_stride, nkv, lam_init=0.0,
           out_init=None):
    wq = q.shape[1] // heads
    wv = v.shape[1] // heads
    n_maps = 2 if kind == "diff" else 1
    q_spec = pl.BlockSpec((tq, wq), lambda b, h, i, j: (q_blk0 + b * q_blk_stride + i, h))
    k_spec = pl.BlockSpec((tk, wq), lambda b, h, i, j: (b * kv_blk_stride + j, h))
    v_spec = pl.BlockSpec((tk, wv), lambda b, h, i, j: (b * kv_blk_stride + j, h))
    o_spec = pl.BlockSpec((tq, wv), lambda b, h, i, j: (q_blk0 + b * q_blk_stride + i, h))
    in_specs = [q_spec, k_spec, v_spec]
    if kind == "diff":
        body = functools.partial(_flash_diff_kernel, lam_init=lam_init)
        in_specs += [pl.BlockSpec(e.shape, lambda b, h, i, j: (0, 0)) for e in extra]
    else:
        body = _flash_mla_kernel
    args = [q, k, v, *extra]
    aliases = {}
    if out_init is not None:
        n_in = len(args)
        in_specs.append(pl.BlockSpec(memory_space=pl.ANY))
        args.append(out_init)
        aliases = {n_in: 0}
        inner = body
        body = lambda *refs: inner(*refs[:n_in], *refs[n_in + 1:])
    return pl.pallas_call(
        body,
        grid=(batch, heads, nq, nkv),
        in_specs=in_specs,
        out_specs=o_spec,
        input_output_aliases=aliases,
        out_shape=jax.ShapeDtypeStruct((q.shape[0], heads * wv), BF16),
        scratch_shapes=([pltpu.VMEM((tq, 1), F32) for _ in range(n_maps)]
                        + [pltpu.VMEM((tq, 2 * wv), F32) for _ in range(n_maps)]),
        compiler_params=pltpu.CompilerParams(
            dimension_semantics=("arbitrary", "arbitrary", "arbitrary", "arbitrary")),
        name="flash_" + kind,
    )(*args)


def _out_epilogue(y, h_ref, g1_ref, nf_ref, sc2_ref, sh2_ref, wrh_ref, wrl_ref, br_ref, hn_ref, v_ref, lg_ref):
    hn = h_ref[...] + g1_ref[...] * y
    hn_ref[...] = hn
    v = _rms(hn, nf_ref[...]) * (1.0 + sc2_ref[...]) + sh2_ref[...]
    _pack_rows(v_ref, v)
    v_hi, v_lo = _split_hi_lo(v)
    wrh = wrh_ref[...]
    lg_ref[...] = _dot(v_hi, wrh) + _dot(v_lo, wrh) + _dot(v_hi, wrl_ref[...]) + br_ref[...]


def _about_kernel(oa_ref, ob_ref, wa_ref, wb_ref, *rest):
    y = _dot(oa_ref[...], wa_ref[...]) + _dot(ob_ref[...], wb_ref[...])
    _out_epilogue(y, *rest)


def _cout_kernel(of_ref, obk_ref, gate_ref, ng_ref, w_ref, *rest):
    o = of_ref[...] + obk_ref[...]
    ng = ng_ref[...]
    parts = [_rms(o[:, c:c + C_HD], ng) for c in range(0, C_HEADS * C_HD, C_HD)]
    g = gate_ref[...]
    x = jnp.concatenate(parts, axis=1) * (g * jax.nn.sigmoid(g))
    _out_epilogue(_dot(x.astype(BF16), w_ref[...]), *rest)


def _outproj(kind, ins, in_specs, h_all, g1, nf, sc2, sh2, wrh, wrl, br, n_lat):
    t_all, d = h_all.shape
    tm = ROW_TILE
    mod = _mod_spec(n_lat // tm)
    row = lambda w: pl.BlockSpec((tm, w), lambda i: (i, 0))
    full = lambda a: pl.BlockSpec(a.shape, lambda i: (0,) * a.ndim)
    nf = nf.reshape(1, d)
    return pl.pallas_call(
        _about_kernel if kind == "ab" else _cout_kernel,
        grid=(t_all // tm,),
        in_specs=in_specs + [row(d), mod, full(nf), mod, mod, full(wrh), full(wrl), full(br)],
        out_specs=[row(d), pl.BlockSpec((tm * SUBLANES, LANES), lambda i: (i, 0)), row(LANES)],
        out_shape=[
            jax.ShapeDtypeStruct((t_all, d), F32),
            jax.ShapeDtypeStruct((t_all * SUBLANES, LANES), F32),
            jax.ShapeDtypeStruct((t_all, LANES), F32),
        ],
        name="outproj_" + kind,
    )(*ins, h_all, g1, nf, sc2, sh2, wrh, wrl, br)


def _cumsum_rows(tri_bf, x):
    hi = x.astype(BF16)
    r1 = x - hi.astype(F32)
    mid = r1.astype(BF16)
    lo = (r1 - mid.astype(F32)).astype(BF16)
    return _dot(tri_bf, hi) + _dot(tri_bf, mid) + _dot(tri_bf, lo)


def _scan_kernel(qf_ref, zf_ref, vf_ref, qb_ref, zb_ref, vb_ref, lb_ref, of_ref, ob_ref, stf_ref, stb_ref):
    _scan_chunk(qf_ref, zf_ref, vf_ref, lb_ref, of_ref, stf_ref, reverse=False)
    _scan_chunk(qb_ref, zb_ref, vb_ref, lb_ref, ob_ref, stb_ref, reverse=True)


def _scan_chunk(q_ref, z_ref, v_ref, lb_ref, o_ref, st_ref, *, reverse):
    c = pl.program_id(1)

    @pl.when(c == 0)
    def _():
        st_ref[...] = jnp.zeros(st_ref.shape, F32)

    L, SB = C_CHUNK, C_SUB
    nsb = L // SB
    lb = lb_ref[...]
    f = lb + (1.0 - lb) * jax.nn.sigmoid(z_ref[...])
    kk = 1.0 - f
    lf = jnp.log(f)
    r_i = lax.broadcasted_iota(jnp.int32, (L, L), 0)
    c_i = lax.broadcasted_iota(jnp.int32, (L, L), 1)
    tri = (c_i >= r_i) if reverse else (c_i <= r_i)
    cum = _cumsum_rows(tri.astype(BF16), lf)
    last_row = 0 if reverse else L - 1
    last = cum[last_row:last_row + 1]
    q = q_ref[...]
    v = v_ref[...]
    qe = (q * jnp.exp(cum)).astype(BF16)
    kdec = (kk * jnp.exp(last - cum)).astype(BF16)
    e_last = jnp.exp(last)
    v_bf = v.astype(BF16)
    ones = jnp.ones((C_HD, C_HD), BF16)
    sub_r = lax.broadcasted_iota(jnp.int32, (SB, C_HD), 0)
    order = list(range(nsb - 1, -1, -1)) if reverse else list(range(nsb))

    outs = []
    for h in range(C_HEADS):
        hs = slice(h * C_HD, (h + 1) * C_HD)
        st = st_ref[h]
        o_h = _dot_nt(qe[:, hs], st.astype(BF16))
        st_ref[h] = st * e_last[:, hs] + _dot_tn(v_bf[:, hs], kdec[:, hs])
        cum_h, q_h, k_h, v_h = cum[:, hs], q[:, hs], kk[:, hs], v[:, hs]
        o_sub = [None] * nsb
        for p, bi in enumerate(order):
            rows = slice(bi * SB, (bi + 1) * SB)
            cum_i, q_i, k_i, v_i = cum_h[rows], q_h[rows], k_h[rows], v_h[rows]
            groups = SB // SUBLANES
            w_rows, spans = [], []
            for s in range(SB):
                g_s = s // SUBLANES
                live = range(0, g_s + 1) if reverse else range(g_s, groups)
                lo, hi = live[0] * SUBLANES, (live[-1] + 1) * SUBLANES
                ok = (sub_r[lo:hi] <= s) if reverse else (sub_r[lo:hi] >= s)
                e = jnp.where(ok, jnp.exp(cum_i[lo:hi] - cum_i[s:s + 1]), 0.0)
                w_rows.append(q_i[lo:hi] * e * k_i[s:s + 1])
                spans.append(live)
            red = _dot(jnp.concatenate(w_rows, axis=0).astype(BF16), ones)
            acc_g = [jnp.zeros((SUBLANES, C_HD), F32) for _ in range(groups)]
            off = 0
            for s in range(SB):
                for g in spans[s]:
                    acc_g[g] = acc_g[g] + red[off:off + SUBLANES] * v_i[s:s + 1]
                    off += SUBLANES
            acc = jnp.concatenate(acc_g, axis=0)
            if p > 0:
                prev = order[p - 1]
                b_row = prev * SB if reverse else prev * SB + SB - 1
                b = cum_h[b_row:b_row + 1]
                if reverse:
                    past = slice((bi + 1) * SB, L)
                else:
                    past = slice(0, bi * SB)
                qi = (q_i * jnp.exp(cum_i - b)).astype(BF16)
                kp = (k_h[past] * jnp.exp(b - cum_h[past])).astype(BF16)
                att = _dot_nt(qi, kp)
                acc = acc + _dot(att.astype(BF16), v_bf[past, hs])
            o_sub[bi] = acc
        outs.append(o_h + jnp.concatenate(o_sub, axis=0))
    o_ref[...] = jnp.concatenate(outs, axis=1)


def _scan(p5, lb, n_lat, n_ctx, batch):
    t_all = p5.shape[0]
    L = C_CHUNK
    w = C_HEADS * C_HD
    nl, nc = n_lat // L, n_ctx // L
    steps = nl + nc

    def row_idx(reverse):
        def idx(b, c):
            if reverse:
                ctx = batch * nl + b * nc + (nc - 1 - c)
                lat = b * nl + (nl - 1 - (c - nc))
            else:
                ctx = batch * nl + b * nc + c
                lat = b * nl + (c - nc)
            return jnp.where(c < nc, ctx, lat)
        return idx

    fwd, bwd = row_idx(False), row_idx(True)
    col = lambda ri, cb: pl.BlockSpec((L, w), lambda b, c: (ri(b, c), cb))
    out = jax.ShapeDtypeStruct((t_all, w), F32)
    state = pltpu.VMEM((C_HEADS, C_HD, C_HD), F32)
    return pl.pallas_call(
        _scan_kernel,
        grid=(batch, steps),
        in_specs=[col(fwd, 0), col(fwd, 1), col(fwd, 3), col(bwd, 0), col(bwd, 2), col(bwd, 3),
                  pl.BlockSpec((1, w), lambda b, c: (0, 0))],
        out_specs=[col(fwd, 0), col(bwd, 0)],
        out_shape=[out, out],
        scratch_shapes=[state, state],
        compiler_params=pltpu.CompilerParams(dimension_semantics=("arbitrary", "arbitrary")),
        name="hgrn_scan",
    )(p5, p5, p5, p5, p5, p5, lb)


def _route_kernel(lg_ref, tri_ref, sel_ref, gate_ref, cnt_ref, carry_ref):
    i = pl.program_id(0)

    @pl.when(i == 0)
    def _():
        carry_ref[...] = jnp.zeros(carry_ref.shape, F32)

    lg = lg_ref[...]
    lane = lax.broadcasted_iota(jnp.int32, lg.shape, 1)
    lane_f = lane.astype(F32)
    x = jnp.where(lane < N_EXPERTS, lg, -jnp.inf)
    vals, idxs, hits = [], [], []
    for _ in range(TOP_K):
        mk = jnp.max(x, axis=-1, keepdims=True)
        ik = jnp.min(jnp.where(x == mk, lane_f, float(LANES)), axis=-1, keepdims=True)
        hit = lane_f == ik
        x = jnp.where(hit, -jnp.inf, x)
        vals.append(mk)
        idxs.append(ik.astype(jnp.int32))
        hits.append(hit)
    member = hits[0] | hits[1] | hits[2] | hits[3]
    member_f = jnp.where(member, 1.0, 0.0)
    before = _dot(tri_ref[...], member_f.astype(BF16)) + carry_ref[0:1]
    carry_ref[0:1] = carry_ref[0:1] + jnp.sum(member_f, axis=0, keepdims=True)
    exps = [jnp.exp(v - vals[0]) for v in vals]
    denom = exps[0] + exps[1] + exps[2] + exps[3]
    sel = jnp.zeros(lg.shape, jnp.int32)
    gate = jnp.zeros(lg.shape, F32)
    for k in range(TOP_K):
        rank = jnp.sum(jnp.where(hits[k], before, 0.0), axis=-1, keepdims=True).astype(jnp.int32)
        sel = jnp.where(lane == k, idxs[k], sel)
        sel = jnp.where(lane == TOP_K + k, rank, sel)
        gate = jnp.where(lane == k, exps[k] / denom, gate)
    sel_ref[...] = sel
    gate_ref[...] = gate
    cnt_ref[...] = carry_ref[...]


def _route(logits):
    t = logits.shape[0]
    tm = ROW_TILE
    tri = (jnp.arange(tm)[:, None] > jnp.arange(tm)[None, :]).astype(BF16)
    blk = pl.BlockSpec((tm, LANES), lambda i: (i, 0))
    return pl.pallas_call(
        _route_kernel,
        grid=(t // tm,),
        in_specs=[blk, pl.BlockSpec((tm, tm), lambda i: (0, 0))],
        out_specs=[blk, blk, pl.BlockSpec((MOD_ROWS, LANES), lambda i: (0, 0))],
        out_shape=[
            jax.ShapeDtypeStruct((t, LANES), jnp.int32),
            jax.ShapeDtypeStruct((t, LANES), F32),
            jax.ShapeDtypeStruct((MOD_ROWS, LANES), F32),
        ],
        scratch_shapes=[pltpu.VMEM((MOD_ROWS, LANES), F32)],
        compiler_params=pltpu.CompilerParams(dimension_semantics=("arbitrary",)),
        name="route",
    )(logits, tri)


def _dispatch_kernel(zs_ref, v_ref, dest_hbm, x_hbm, idx_smem, zbuf, sems):
    i = pl.program_id(0)
    tm = v_ref.shape[0] // SUBLANES
    pad = zbuf.shape[0]

    @pl.when(i == 0)
    def _():
        zbuf[...] = jnp.zeros(zbuf.shape, F32)
        for e in range(N_EXPERTS):
            z0 = pl.multiple_of(zs_ref[e] * SUBLANES, SUBLANES)
            pltpu.make_async_copy(zbuf, x_hbm.at[pl.ds(z0, pad), :], sems.at[2]).start()
        for e in range(N_EXPERTS):
            pltpu.make_async_copy(zbuf, x_hbm.at[pl.ds(0, pad), :], sems.at[2]).wait()

    ids = pltpu.make_async_copy(dest_hbm.at[i], idx_smem, sems.at[0])
    ids.start()
    ids.wait()

    def issue(r, carry):
        for k in range(TOP_K):
            _row_copy(v_ref, r, x_hbm, idx_smem[r * TOP_K + k], sems.at[1]).start(priority=k % 2)
        return carry

    lax.fori_loop(0, tm, issue, 0, unroll=4)

    def drain(r, carry):
        _row_copy(v_ref, 0, x_hbm, 0, sems.at[1]).wait()
        return carry

    lax.fori_loop(0, tm * TOP_K, drain, 0, unroll=8)


def _dispatch(v, dest, zero_start, n_rows):
    t = v.shape[0] // SUBLANES
    tm = COMBINE_TILE
    grid_spec = pltpu.PrefetchScalarGridSpec(
        num_scalar_prefetch=1,
        grid=(t // tm,),
        in_specs=[pl.BlockSpec((tm * SUBLANES, LANES), lambda i, zs: (i, 0)), pl.BlockSpec(memory_space=pl.ANY)],
        out_specs=pl.BlockSpec(memory_space=pl.ANY),
        scratch_shapes=[pltpu.SMEM((tm * TOP_K,), jnp.int32), pltpu.VMEM((MOE_TILE * SUBLANES, LANES), F32),
                        pltpu.SemaphoreType.DMA((3,))],
    )
    return pl.pallas_call(
        _dispatch_kernel,
        grid_spec=grid_spec,
        out_shape=jax.ShapeDtypeStruct(((n_rows + MOE_TILE) * SUBLANES, LANES), F32),
        compiler_params=pltpu.CompilerParams(dimension_semantics=("arbitrary",)),
        name="dispatch",
    )(zero_start, v, dest)


def _moe_kernel(be_ref, nu_ref, first_ref, x_ref, w1_ref, b1g_ref, b1l_ref, w2_ref, b2_ref, perm_ref, o_ref,
                w1g_scr, w1l_scr, w2_scr):
    i = pl.program_id(0)
    live = i < nu_ref[0]

    @pl.when(jnp.logical_and(live, first_ref[i] == 1))
    def _():
        perm = perm_ref[...]
        group = 2 * LANES
        for b in range(w1_ref.shape[1] // group):
            r = _dot(w1_ref[:, b * group:(b + 1) * group].astype(BF16), perm)
            w1g_scr[:, b * LANES:(b + 1) * LANES] = r[:, :LANES].astype(BF16)
            w1l_scr[:, b * LANES:(b + 1) * LANES] = r[:, LANES:].astype(BF16)
        w2_scr[...] = w2_ref[...].astype(BF16)

    @pl.when(live)
    def _():
        x = _unpack_rows(x_ref, MOE_TILE).astype(BF16)
        hg = _dot(x, w1g_scr[...]) + b1g_ref[...]
        hl = _dot(x, w1l_scr[...]) + b1l_ref[...]
        glu = jnp.minimum(hg, SWIGLU_LIMIT)
        lin = jnp.clip(hl, -SWIGLU_LIMIT, SWIGLU_LIMIT)
        y = glu * jax.nn.sigmoid(SWIGLU_ALPHA * glu) * (lin + 1.0)
        _pack_rows(o_ref, _dot(y.astype(BF16), w2_scr[...]) + b2_ref[...])

    @pl.when(jnp.logical_not(live))
    def _():
        o_ref[...] = jnp.zeros(o_ref.shape, F32)


def _deinterleave_perm():
    src = jnp.arange(2 * LANES)
    dst = jnp.where(src % 2 == 0, src // 2, LANES + src // 2)
    return (dst[:, None] == jnp.arange(2 * LANES)[None, :]).astype(BF16)


def _moe_experts(xb, n_rows, block_expert, n_used, first, layer, w1, b1g, b1l, w2, b2):
    d = w1.shape[2]
    tm = MOE_TILE
    dff = w2.shape[2]
    perm = _deinterleave_perm()
    tile = pl.BlockSpec((tm * SUBLANES, LANES), lambda i, be, nu, fi: (i, 0))
    wspec = lambda k, n: pl.BlockSpec((None, k, n), lambda i, be, nu, fi: (be[i], 0, 0))
    lwspec = lambda k, n: pl.BlockSpec((None, None, k, n), lambda i, be, nu, fi: (layer, be[i], 0, 0))
    grid_spec = pltpu.PrefetchScalarGridSpec(
        num_scalar_prefetch=3,
        grid=(n_rows // tm,),
        in_specs=[
            tile,
            lwspec(d, 2 * dff), wspec(1, dff), wspec(1, dff), lwspec(dff, d), wspec(1, d),
            pl.BlockSpec(perm.shape, lambda i, be, nu, fi: (0, 0)),
        ],
        out_specs=tile,
        scratch_shapes=[pltpu.VMEM((d, dff), BF16), pltpu.VMEM((d, dff), BF16), pltpu.VMEM((dff, d), BF16)],
    )
    return pl.pallas_call(
        _moe_kernel,
        grid_spec=grid_spec,
        out_shape=jax.ShapeDtypeStruct((n_rows * SUBLANES, LANES), F32),
        compiler_params=pltpu.CompilerParams(dimension_semantics=("arbitrary",), vmem_limit_bytes=MOE_VMEM_BYTES),
        name="moe_experts",
    )(block_expert, n_used, first, xb, w1, b1g, b1l, w2, b2, perm)


def _moe(v, logits, layer, w1, b1g, b1l, w2, b2):
    t = logits.shape[0]
    tm = MOE_TILE
    sel, gates, cnt = _route(logits)
    counts = cnt[0, :N_EXPERTS].astype(jnp.int32)
    padded = (counts + tm - 1) // tm * tm
    pad_end = jnp.cumsum(padded)
    pad_start = pad_end - padded
    n_blocks = t * TOP_K // tm + N_EXPERTS
    n_rows = n_blocks * tm
    block_start = jnp.arange(n_blocks, dtype=jnp.int32) * tm
    block_expert = jnp.minimum(jnp.sum(block_start[:, None] >= pad_end[None, :], axis=1), N_EXPERTS - 1)
    block_expert = block_expert.astype(jnp.int32)
    n_used = (pad_end[-1] // tm).astype(jnp.int32).reshape(1)
    first = jnp.concatenate([jnp.ones((1,), jnp.int32), (block_expert[1:] != block_expert[:-1]).astype(jnp.int32)])
    expert = sel[:, :TOP_K]
    base = jnp.sum(jnp.where(expert[:, :, None] == jnp.arange(N_EXPERTS)[None, None, :], pad_start[None, None, :], 0),
                   axis=-1)
    dest = (base + sel[:, TOP_K:2 * TOP_K]).astype(jnp.int32).reshape(t // COMBINE_TILE, COMBINE_TILE * TOP_K)
    x_sorted = _dispatch(v, dest, (pad_start + counts).astype(jnp.int32), n_rows)
    y_sorted = _moe_experts(x_sorted, n_rows, block_expert, n_used, first, layer, w1, b1g, b1l, w2, b2)
    return gates, dest, y_sorted


def _final_kernel(h_ref, g2_ref, gate_ref, dest_hbm, y_hbm, g_ref, o_ref, idx_smem, ybuf, sems):
    moe = _moe_combine(pl.program_id(0), dest_hbm, y_hbm, gate_ref, idx_smem, ybuf, sems)
    o_ref[...] = _rms(h_ref[...] + g2_ref[...] * moe, g_ref[...])


def _final_norm(h_all, pending, g, t_lat, n_lat):
    d = h_all.shape[1]
    tm = COMBINE_TILE
    g2, gates, dest, y_sorted = pending
    row = pl.BlockSpec((tm, d), lambda i: (i, 0))
    any_spec = pl.BlockSpec(memory_space=pl.ANY)
    return pl.pallas_call(
        _final_kernel,
        grid=(t_lat // tm,),
        in_specs=[row, _mod_spec(n_lat // tm), pl.BlockSpec((tm, LANES), lambda i: (i, 0)), any_spec, any_spec,
                  pl.BlockSpec((1, d), lambda i: (0, 0))],
        out_specs=row,
        out_shape=jax.ShapeDtypeStruct((t_lat, d), F32),
        scratch_shapes=_combine_scratch(tm, d),
        compiler_params=pltpu.CompilerParams(dimension_semantics=("arbitrary",)),
        name="final_norm",
    )(h_all, g2, gates, dest, y_sorted, g.reshape(1, d))


def _rope_tables(n_lat, n_ctx_pad):
    quarter = A_HD // 4
    inv_freq = ROPE_THETA ** (-jnp.arange(quarter, dtype=F32) / quarter)
    t = jnp.arange(n_lat)
    row = (t // GRID_W).astype(F32)
    colp = (t % GRID_W).astype(F32)
    ang_r = row[:, None] * inv_freq
    ang_c = colp[:, None] * inv_freq
    cos64 = jnp.concatenate([jnp.cos(ang_r), jnp.cos(ang_r), jnp.cos(ang_c), jnp.cos(ang_c)], axis=1)
    sin64 = jnp.concatenate([-jnp.sin(ang_r), jnp.sin(ang_r), -jnp.sin(ang_c), jnp.sin(ang_c)], axis=1)
    one, zero = jnp.ones_like(cos64), jnp.zeros_like(sin64)

    def pad(a, fill):
        return jnp.concatenate([a, jnp.full((n_ctx_pad, a.shape[1]), fill, F32)], axis=0)

    cos_a = pad(jnp.concatenate([cos64, cos64], axis=1), 1.0)
    sin_a = pad(jnp.concatenate([sin64, sin64], axis=1), 0.0)
    cos_b = pad(jnp.concatenate([cos64, one], axis=1), 1.0)
    sin_b = pad(jnp.concatenate([sin64, zero], axis=1), 0.0)
    return cos_a, sin_a, cos_b, sin_b


def _prep_uq(w_uq):
    r = w_uq.shape[0]
    w = w_uq.reshape(r, B_HEADS, B_NOPE + B_ROPE)
    w = jnp.pad(w, ((0, 0), (0, 0), (0, 2 * LANES - B_NOPE - B_ROPE)))
    return w.reshape(r, B_HEADS * 2 * LANES).astype(BF16)


def _prep_ukv(w_ukv):
    r = w_ukv.shape[0]
    w = w_ukv.reshape(r, B_HEADS, B_NOPE + B_VD)
    wk = w[:, :, :B_NOPE].reshape(r, B_HEADS * B_NOPE)
    wv = w[:, :, B_NOPE:].reshape(r, B_HEADS * B_VD)
    return wk.astype(BF16), wv.astype(BF16)


def kernel(x, c, ctx, c_ctx, norm_mix_g, norm_ffn_g, w_ada, b_ada, w_in_ab, diff_lambda, diff_subln_g, mla_q_norm_g, mla_kv_norm_g, w_uq, w_ukv, w_out_ab, w_in_c, lb_raw, hgrn_norm_g, w_out_c, w_router, b_router, w_exp1, b_exp1, w_exp2, b_exp2, final_g):
    batch, n_lat, d = x.shape
    n_ctx = ctx.shape[1]
    t_lat = batch * n_lat
    h_all = jnp.concatenate([x.reshape(t_lat, d), ctx.reshape(batch * n_ctx, d)], axis=0)

    cvec = jnp.concatenate([c, c_ctx[None, :], jnp.zeros((MOD_ROWS - batch - 1, d), F32)], axis=0)
    mods = _ada(cvec, w_ada, b_ada)

    lb_p = jax.nn.softmax(lb_raw.astype(F32), axis=0)
    lower_bounds = jnp.cumsum(lb_p, axis=0) - lb_p[0]
    tabs = _rope_tables(n_lat, PREP_TILE)

    kv_len = n_lat + n_ctx
    tq = min(1024, n_lat)
    tk = next(t for t in (768, 512, 256) if kv_len % t == 0)

    pending = None
    for l in range(DEPTH):
        j = l // 2
        m = mods[l]
        sh1, sc1, g1, sh2, sc2, g2 = [m[:, k * d:(k + 1) * d].reshape(MOD_ROWS, 1, d) for k in range(6)]
        wr = jnp.pad(w_router[l], ((0, 0), (0, LANES - N_EXPERTS)))
        wrh, wrl = _split_hi_lo(wr)
        br = jnp.pad(b_router[l].astype(F32), (0, LANES - N_EXPERTS)).reshape(1, LANES)
        row = lambda w: pl.BlockSpec((ROW_TILE, w), lambda i: (i, 0))
        full = lambda a: pl.BlockSpec(a.shape, lambda i: (0,) * a.ndim)
        if l % 2 == 0:
            w_in, tn = jnp.pad(w_in_ab[j], ((0, 0), (0, AB_PROJ_PAD - w_in_ab.shape[2]))).astype(BF16), AB_PROJ_PAD
        else:
            w_in, tn = w_in_c[j].astype(BF16), 1024
        p, h_all = _inproj(h_all, norm_mix_g[l], sc1, sh1, w_in, n_lat, tn, pending)
        tail = (h_all, g1, norm_ffn_g[l], sc2, sh2, wrh, wrl, br, n_lat)
        if l % 2 == 0:
            lam_init = 0.8 - 0.6 * math.exp(-0.3 * l)
            wuk, wuv = _prep_ukv(w_ukv[j])
            qa, qb, ka, va, kb, vb = _abprep(
                p, tabs, mla_q_norm_g[j].reshape(1, -1), mla_kv_norm_g[j].reshape(1, -1), _prep_uq(w_uq[j]), wuk, wuv,
                n_lat, n_ctx, batch)
            extra = (diff_lambda[j].astype(F32), diff_subln_g[j].reshape(1, A_VD))
            lat = dict(batch=batch, tq=tq, tk=tk, q_blk0=0, q_blk_stride=n_lat // tq, nq=n_lat // tq,
                       kv_blk_stride=kv_len // tk, nkv=kv_len // tk)
            oa = _flash("diff", qa, ka, va, extra, heads=A_HEADS, lam_init=lam_init, **lat)
            ob = _flash("mla", qb, kb, vb, (), heads=B_HEADS, **lat)
            if l != DEPTH - 1:
                cq = dict(batch=batch, tq=n_ctx, tk=n_ctx, q_blk0=t_lat // n_ctx, q_blk_stride=1, nq=1,
                          kv_blk_stride=kv_len // n_ctx, nkv=1)
                oa = _flash("diff", qa, ka, va, extra, heads=A_HEADS, lam_init=lam_init, out_init=oa, **cq)
                ob = _flash("mla", qb, kb, vb, (), heads=B_HEADS, out_init=ob, **cq)
            wo = w_out_ab[j].astype(BF16)
            wa, wb = wo[:A_HEADS * A_VD], wo[A_HEADS * A_VD:]
            h_mid, v_ffn, logits = _outproj(
                "ab", (oa, ob, wa, wb), [row(oa.shape[1]), row(ob.shape[1]), full(wa), full(wb)], *tail)
        else:
            p5 = p
            lb = lower_bounds[l].reshape(1, -1)
            o_f, o_b = _scan(p5, lb, n_lat, n_ctx, batch)
            ng = hgrn_norm_g[j].reshape(1, C_HD)
            wo = w_out_c[j].astype(BF16)
            gate_spec = pl.BlockSpec((ROW_TILE, d), lambda i: (i, 4))
            h_mid, v_ffn, logits = _outproj(
                "c", (o_f, o_b, p5, ng, wo), [row(d), row(d), gate_spec, full(ng), full(wo)], *tail)

        b1 = b_exp1[l].astype(F32)
        b1g, b1l = b1[:, None, 0::2], b1[:, None, 1::2]
        pending = (g2,) + _moe(v_ffn, logits, l, w_exp1, b1g, b1l, w_exp2, b_exp2[l].astype(F32)[:, None, :])
        h_all = h_mid

    return _final_norm(h_all, pending, final_g, t_lat, n_lat).reshape(batch, n_lat, d)
```

```python
import functools
import math

import jax
import jax.numpy as jnp
from jax import lax
from jax.experimental import pallas as pl
from jax.experimental.pallas import tpu as pltpu

F32 = jnp.float32
BF16 = jnp.bfloat16

D_MODEL = 1024
DEPTH = 4
GRID_W = 64
ROPE_THETA = 10000.0
NORM_EPS = 1e-6

A_HEADS = 4
A_HD = 64
A_VD = 128
A_SCALE = A_HD ** -0.5
B_HEADS = 4
B_NOPE = 128
B_ROPE = 64
B_VD = 128
B_Q_RANK = 256
B_KV_RANK = 128
B_SCALE = (B_NOPE + B_ROPE) ** -0.5
LOG2E = math.log2(math.e)
AB_PROJ_PAD = 2048

C_HEADS = 8
C_HD = 128
C_CHUNK = 64
C_SUB = 16

N_EXPERTS = 32
TOP_K = 4
SWIGLU_ALPHA = 1.702
SWIGLU_LIMIT = 7.0

LANES = 128
SUBLANES = 8
ROW_TILE = 512
PREP_TILE = 256
FLASH_SUB = 256
MOE_TILE = 256
COMBINE_TILE = 256
MOD_ROWS = 8
MOE_VMEM_BYTES = 52 * 1024 * 1024


def _split_hi_lo(x):
    hi = x.astype(BF16)
    lo = (x - hi.astype(F32)).astype(BF16)
    return hi, lo


def _dot(a, b):
    return jnp.dot(a, b, preferred_element_type=F32)


def _dot_nt(a, b):
    return lax.dot_general(a, b, (((1,), (1,)), ((), ())), preferred_element_type=F32)


def _dot_tn(a, b):
    return lax.dot_general(a, b, (((0,), (0,)), ((), ())), preferred_element_type=F32)


def _rms(x, g):
    ms = jnp.mean(x * x, axis=-1, keepdims=True)
    return x * lax.rsqrt(ms + NORM_EPS) * g


def _ada_kernel(c_ref, w_ref, b_ref, o_ref):
    c = c_ref[...]
    s = c * jax.nn.sigmoid(c)
    s_hi, s_lo = _split_hi_lo(s)
    w_hi, w_lo = _split_hi_lo(w_ref[...])
    o_ref[...] = _dot(s_hi, w_hi) + _dot(s_lo, w_hi) + _dot(s_hi, w_lo) + b_ref[...]


def _ada(cvec, w_ada, b_ada):
    depth, d, n6 = w_ada.shape
    tn = 1536
    return pl.pallas_call(
        _ada_kernel,
        grid=(depth, n6 // tn),
        in_specs=[
            pl.BlockSpec((MOD_ROWS, d), lambda l, j: (0, 0)),
            pl.BlockSpec((None, d, tn), lambda l, j: (l, 0, j)),
            pl.BlockSpec((None, 1, tn), lambda l, j: (l, 0, j)),
        ],
        out_specs=pl.BlockSpec((None, MOD_ROWS, tn), lambda l, j: (l, 0, j)),
        out_shape=jax.ShapeDtypeStruct((depth, MOD_ROWS, n6), F32),
        name="ada_mod",
    )(cvec, w_ada, b_ada.reshape(depth, 1, n6))


def _unpack_rows(ref, tm):
    return jnp.concatenate([ref[pl.ds(s, tm, stride=SUBLANES), :] for s in range(SUBLANES)], axis=1)


def _pack_rows(ref, x):
    tm = x.shape[0]
    for s in range(SUBLANES):
        ref[pl.ds(s, tm, stride=SUBLANES), :] = x[:, s * LANES:(s + 1) * LANES]


RUN_ROW = COMBINE_TILE * TOP_K
RUN_LEN = RUN_ROW + N_EXPERTS
RUN_TABLE = RUN_ROW + LANES
RUN_BITS = tuple(1 << b for b in range(COMBINE_TILE.bit_length() - 1, -1, -1))


def _fetch_table(i, tab_hbm, tab_smem, sem):
    cp = pltpu.make_async_copy(tab_hbm.at[i], tab_smem, sem)
    cp.start()
    cp.wait()


def _run_copies(tab_smem, hbm, slab, sem, *, to_hbm, wait):
    def per_expert(e, cur):
        row0 = tab_smem[RUN_ROW + e]
        n = tab_smem[RUN_LEN + e]
        pos = jnp.int32(0)
        for bit in RUN_BITS:
            take = (n & bit) != 0

            @pl.when(take)
            def _():
                h = hbm.at[pl.ds(pl.multiple_of((row0 + pos) * SUBLANES, SUBLANES), bit * SUBLANES), :]
                s = slab.at[pl.ds(pl.multiple_of((cur + pos) * SUBLANES, SUBLANES), bit * SUBLANES), :]
                cp = pltpu.make_async_copy(s, h, sem) if to_hbm else pltpu.make_async_copy(h, s, sem)
                if wait:
                    cp.wait()
                else:
                    cp.start()

            pos = pos + jnp.where(take, bit, 0)
        return cur + n

    lax.fori_loop(0, N_EXPERTS, per_expert, jnp.int32(0))


def _moe_combine(i, tab_hbm, y_hbm, gate_ref, tab_smem, slab, ybuf, sems):
    tm = ybuf.shape[1] // SUBLANES
    _fetch_table(i, tab_hbm, tab_smem, sems.at[0])
    _run_copies(tab_smem, y_hbm, slab, sems.at[1], to_hbm=False, wait=False)
    _run_copies(tab_smem, y_hbm, slab, sems.at[1], to_hbm=False, wait=True)

    def place(r, carry):
        for k in range(TOP_K):
            src = pl.multiple_of(tab_smem[r * TOP_K + k] * SUBLANES, SUBLANES)
            ybuf[k, pl.ds(pl.multiple_of(r * SUBLANES, SUBLANES), SUBLANES), :] = slab[pl.ds(src, SUBLANES), :]
        return carry

    lax.fori_loop(0, tm, place, 0, unroll=4)
    gates = gate_ref[...]
    acc = gates[:, 0:1] * _unpack_rows(ybuf.at[0], tm)
    for k in range(1, TOP_K):
        acc = acc + gates[:, k:k + 1] * _unpack_rows(ybuf.at[k], tm)
    return acc


def _inproj_kernel(h_ref, g_ref, sc_ref, sh_ref, w_ref, o_ref, u_scr):
    @pl.when(pl.program_id(1) == 0)
    def _():
        u = _rms(h_ref[...], g_ref[...]) * (1.0 + sc_ref[...]) + sh_ref[...]
        u_scr[...] = u.astype(BF16)

    o_ref[...] = _dot(u_scr[...], w_ref[...])


def _inproj_moe_kernel(h_ref, g2_ref, gate_ref, tab_hbm, y_hbm, g_ref, sc_ref, sh_ref, w_ref, o_ref, hn_ref,
                       u_scr, *combine_scratch):
    @pl.when(pl.program_id(1) == 0)
    def _():
        moe = _moe_combine(pl.program_id(0), tab_hbm, y_hbm, gate_ref, *combine_scratch)
        h = h_ref[...] + g2_ref[...] * moe
        hn_ref[...] = h
        u = _rms(h, g_ref[...]) * (1.0 + sc_ref[...]) + sh_ref[...]
        u_scr[...] = u.astype(BF16)

    o_ref[...] = _dot(u_scr[...], w_ref[...])


def _mod_spec(n_lat_blocks):
    def idx(i, *_):
        return (jnp.minimum(i // n_lat_blocks, 2), 0, 0)

    return pl.BlockSpec((None, 1, D_MODEL), idx)


def _combine_scratch(tm, d):
    assert d == SUBLANES * LANES and tm == COMBINE_TILE
    return [pltpu.SMEM((RUN_TABLE,), jnp.int32), pltpu.VMEM((TOP_K * tm * SUBLANES, LANES), F32),
            pltpu.VMEM((TOP_K, tm * SUBLANES, LANES), F32), pltpu.SemaphoreType.DMA((2,))]


def _inproj(h_all, gain, sc, sh, w_bf, n_lat, tn, pending=None):
    t_all, d = h_all.shape
    nout = w_bf.shape[1]
    tm = COMBINE_TILE
    mod = _mod_spec(n_lat // tm)
    row = pl.BlockSpec((tm, d), lambda i, j: (i, 0))
    tail_specs = [pl.BlockSpec((1, d), lambda i, j: (0, 0)), mod, mod, pl.BlockSpec((d, tn), lambda i, j: (0, j))]
    tail = (gain.reshape(1, d), sc, sh, w_bf)
    p_spec = pl.BlockSpec((tm, tn), lambda i, j: (i, j))
    p_shape = jax.ShapeDtypeStruct((t_all, nout), F32)
    params = pltpu.CompilerParams(dimension_semantics=("arbitrary", "arbitrary"))
    if pending is None:
        p = pl.pallas_call(
            _inproj_kernel,
            grid=(t_all // tm, nout // tn),
            in_specs=[row] + tail_specs,
            out_specs=p_spec,
            out_shape=p_shape,
            scratch_shapes=[pltpu.VMEM((tm, d), BF16)],
            compiler_params=params,
            name="inproj",
        )(h_all, *tail)
        return p, h_all
    g2, gates, dest, y_sorted = pending
    any_spec = pl.BlockSpec(memory_space=pl.ANY)
    return pl.pallas_call(
        _inproj_moe_kernel,
        grid=(t_all // tm, nout // tn),
        in_specs=[row, mod, pl.BlockSpec((tm, LANES), lambda i, j: (i, 0)), any_spec, any_spec] + tail_specs,
        out_specs=[p_spec, row],
        out_shape=[p_shape, jax.ShapeDtypeStruct((t_all, d), F32)],
        scratch_shapes=[pltpu.VMEM((tm, d), BF16)] + _combine_scratch(tm, d),
        compiler_params=params,
        name="inproj_moe",
    )(h_all, g2, gates, dest, y_sorted, *tail)


def _rope(x, cos, sin):
    n = x.shape[-1]
    lane = lax.broadcasted_iota(jnp.int32, x.shape, 1)
    first = (lane // 16) % 2 == 0
    partner = jnp.where(first, pltpu.roll(x, n - 16, 1), pltpu.roll(x, 16, 1))
    return x * cos + partner * sin


def _abprep_kernel(p_ref, cosa_ref, sina_ref, cosb_ref, sinb_ref, qg_ref, kvg_ref, wuq_ref, wuk_ref, wuv_ref,
                   qa_ref, qb_ref, ka_ref, va_ref, kb_ref, vb_ref):
    cosa, sina = cosa_ref[...], sina_ref[...]
    cosb, sinb = cosb_ref[...], sinb_ref[...]
    n_a = 2 * A_HEADS * A_HD
    qa = [_rope(p_ref[:, c:c + LANES], cosa, sina) * (A_SCALE * LOG2E) for c in range(0, n_a, LANES)]
    qa_ref[...] = jnp.concatenate(qa, axis=1).astype(BF16)
    ka = [_rope(p_ref[:, n_a + c:n_a + c + LANES], cosa, sina) for c in range(0, n_a, LANES)]
    ka_ref[...] = jnp.concatenate(ka, axis=1).astype(BF16)
    va_ref[...] = p_ref[:, 2 * n_a:2 * n_a + A_HEADS * A_VD].astype(BF16)

    off = 2 * n_a + A_HEADS * A_VD
    cq = _rms(p_ref[:, off:off + B_Q_RANK], qg_ref[...]).astype(BF16)
    qf = _dot(cq, wuq_ref[...])
    ckv = _rms(p_ref[:, off + B_Q_RANK:off + B_Q_RANK + B_KV_RANK], kvg_ref[...]).astype(BF16)
    kn = _dot(ckv, wuk_ref[...])
    vb_ref[...] = _dot(ckv, wuv_ref[...]).astype(BF16)
    kr_off = off + B_Q_RANK + B_KV_RANK
    krr = _rope(p_ref[:, kr_off:kr_off + LANES], cosb, sinb)
    qb, kb = [], []
    for h in range(B_HEADS):
        qb.append(qf[:, 2 * LANES * h:2 * LANES * h + LANES] * (B_SCALE * LOG2E))
        qb.append(_rope(qf[:, 2 * LANES * h + LANES:2 * LANES * (h + 1)], cosb, sinb) * (B_SCALE * LOG2E))
        kb.append(kn[:, LANES * h:LANES * (h + 1)])
        kb.append(krr)
    qb_ref[...] = jnp.concatenate(qb, axis=1).astype(BF16)
    kb_ref[...] = jnp.concatenate(kb, axis=1).astype(BF16)


def _abprep(p, tabs, qg, kvg, wuq, wuk, wuv, n_lat, n_ctx, batch):
    t_all = p.shape[0]
    tm = PREP_TILE
    nlb = n_lat // tm
    ncb = n_ctx // tm
    kvb = nlb + ncb

    def tab_idx(i):
        return (jnp.where(i < batch * nlb, i % nlb, nlb), 0)

    def kv_idx(i):
        lat = (i // nlb) * kvb + ncb + i % nlb
        j = i - batch * nlb
        ctx = (j // ncb) * kvb + j % ncb
        return (jnp.where(i < batch * nlb, lat, ctx), 0)

    tab_spec = pl.BlockSpec((tm, LANES), tab_idx)
    full = lambda a: pl.BlockSpec(a.shape, lambda i: (0,) * a.ndim)
    wq, wk, wv = 2 * A_HEADS * A_HD, B_HEADS * 2 * LANES, A_HEADS * A_VD
    kv_rows = batch * (n_lat + n_ctx)
    return pl.pallas_call(
        _abprep_kernel,
        grid=(t_all // tm,),
        in_specs=[pl.BlockSpec((tm, AB_PROJ_PAD), lambda i: (i, 0)), tab_spec, tab_spec, tab_spec, tab_spec,
                  full(qg), full(kvg), full(wuq), full(wuk), full(wuv)],
        out_specs=[
            pl.BlockSpec((tm, wq), lambda i: (i, 0)),
            pl.BlockSpec((tm, wk), lambda i: (i, 0)),
            pl.BlockSpec((tm, wq), kv_idx),
            pl.BlockSpec((tm, wv), kv_idx),
            pl.BlockSpec((tm, wk), kv_idx),
            pl.BlockSpec((tm, wv), kv_idx),
        ],
        out_shape=[
            jax.ShapeDtypeStruct((t_all, wq), BF16),
            jax.ShapeDtypeStruct((t_all, wk), BF16),
            jax.ShapeDtypeStruct((kv_rows, wq), BF16),
            jax.ShapeDtypeStruct((kv_rows, wv), BF16),
            jax.ShapeDtypeStruct((kv_rows, wk), BF16),
            jax.ShapeDtypeStruct((kv_rows, wv), BF16),
        ],
        name="ab_prep",
    )(p, *tabs, qg, kvg, wuq, wuk, wuv)


def _flash_steps(j, q_maps, k_ref, v_ref, m_refs, acc_refs):
    @pl.when(j == 0)
    def _():
        for m_ref, acc_ref in zip(m_refs, acc_refs):
            m_ref[...] = jnp.full(m_ref.shape, -jnp.inf, F32)
            acc_ref[...] = jnp.zeros(acc_ref.shape, F32)

    k = k_ref[...]
    v = v_ref[...]
    v_ext = jnp.concatenate([v, jnp.ones(v.shape, BF16)], axis=1)
    tq = m_refs[0].shape[0]
    sub = min(FLASH_SUB, tq)
    for r0 in range(0, tq, sub):
        rows = slice(r0, r0 + sub)
        for q_of, m_ref, acc_ref in zip(q_maps, m_refs, acc_refs):
            s = _dot_nt(q_of(rows), k)
            m_prev = m_ref[rows, :]
            m_new = jnp.maximum(m_prev, jnp.max(s, axis=-1, keepdims=True))
            p = jnp.exp2(s - m_new).astype(BF16)
            acc_ref[rows, :] = jnp.exp2(m_prev - m_new) * acc_ref[rows, :] + _dot(p, v_ext)
            m_ref[rows, :] = m_new


def _flash_result(acc_ref):
    wv = acc_ref.shape[1] // 2
    return acc_ref[:, :wv] / acc_ref[:, wv:]


def _flash_diff_kernel(q_ref, k_ref, v_ref, lam_ref, g_ref, o_ref, m1_ref, m2_ref, acc1_ref, acc2_ref, *, lam_init):
    j = pl.program_id(3)

    def q_half(upper):
        def get(rows):
            q = q_ref[rows, :]
            lane = lax.broadcasted_iota(jnp.int32, q.shape, 1)
            return jnp.where((lane >= A_HD) == upper, q, jnp.zeros_like(q))
        return get

    _flash_steps(j, (q_half(False), q_half(True)), k_ref, v_ref, (m1_ref, m2_ref), (acc1_ref, acc2_ref))

    @pl.when(j == pl.num_programs(3) - 1)
    def _():
        lf = lam_ref[...]
        lam = (jnp.exp(jnp.sum(lf[0:1] * lf[1:2], axis=-1, keepdims=True))
               - jnp.exp(jnp.sum(lf[2:3] * lf[3:4], axis=-1, keepdims=True)) + lam_init)
        o = _flash_result(acc1_ref) - lam * _flash_result(acc2_ref)
        o_ref[...] = (_rms(o, g_ref[...]) * (1.0 - lam_init)).astype(o_ref.dtype)


def _flash_mla_kernel(q_ref, k_ref, v_ref, o_ref, m_ref, acc_ref):
    j = pl.program_id(3)
    _flash_steps(j, (lambda rows: q_ref[rows, :],), k_ref, v_ref, (m_ref,), (acc_ref,))

    @pl.when(j == pl.num_programs(3) - 1)
    def _():
        o_ref[...] = _flash_result(acc_ref).astype(o_ref.dtype)


def _flash(kind, q, k, v, extra, *, batch, heads, tq, tk, q_blk0, q_blk_stride, nq, kv_blk_stride, nkv, lam_init=0.0,
           out_init=None):
    wq = q.shape[1] // heads
    wv = v.shape[1] // heads
    n_maps = 2 if kind == "diff" else 1
    q_spec = pl.BlockSpec((tq, wq), lambda b, h, i, j: (q_blk0 + b * q_blk_stride + i, h))
    k_spec = pl.BlockSpec((tk, wq), lambda b, h, i, j: (b * kv_blk_stride + j, h))
    v_spec = pl.BlockSpec((tk, wv), lambda b, h, i, j: (b * kv_blk_stride + j, h))
    o_spec = pl.BlockSpec((tq, wv), lambda b, h, i, j: (q_blk0 + b * q_blk_stride + i, h))
    in_specs = [q_spec, k_spec, v_spec]
    if kind == "diff":
        body = functools.partial(_flash_diff_kernel, lam_init=lam_init)
        in_specs += [pl.BlockSpec(e.shape, lambda b, h, i, j: (0, 0)) for e in extra]
    else:
        body = _flash_mla_kernel
    args = [q, k, v, *extra]
    aliases = {}
    if out_init is not None:
        n_in = len(args)
        in_specs.append(pl.BlockSpec(memory_space=pl.ANY))
        args.append(out_init)
        aliases = {n_in: 0}
        inner = body
        body = lambda *refs: inner(*refs[:n_in], *refs[n_in + 1:])
    return pl.pallas_call(
        body,
        grid=(batch, heads, nq, nkv),
        in_specs=in_specs,
        out_specs=o_spec,
        input_output_aliases=aliases,
        out_shape=jax.ShapeDtypeStruct((q.shape[0], heads * wv), BF16),
        scratch_shapes=([pltpu.VMEM((tq, 1), F32) for _ in range(n_maps)]
                        + [pltpu.VMEM((tq, 2 * wv), F32) for _ in range(n_maps)]),
        compiler_params=pltpu.CompilerParams(
            dimension_semantics=("arbitrary", "arbitrary", "arbitrary", "arbitrary")),
        name="flash_" + kind,
    )(*args)


def _out_epilogue(y, h_ref, g1_ref, nf_ref, sc2_ref, sh2_ref, wrh_ref, wrl_ref, br_ref, hn_ref, v_ref, lg_ref):
    hn = h_ref[...] + g1_ref[...] * y
    hn_ref[...] = hn
    v = _rms(hn, nf_ref[...]) * (1.0 + sc2_ref[...]) + sh2_ref[...]
    _pack_rows(v_ref, v)
    v_hi, v_lo = _split_hi_lo(v)
    wrh = wrh_ref[...]
    lg_ref[...] = _dot(v_hi, wrh) + _dot(v_lo, wrh) + _dot(v_hi, wrl_ref[...]) + br_ref[...]


def _about_kernel(oa_ref, ob_ref, wa_ref, wb_ref, *rest):
    y = _dot(oa_ref[...], wa_ref[...]) + _dot(ob_ref[...], wb_ref[...])
    _out_epilogue(y, *rest)


def _cout_kernel(of_ref, obk_ref, gate_ref, ng_ref, w_ref, *rest):
    o = of_ref[...] + obk_ref[...]
    ng = ng_ref[...]
    parts = [_rms(o[:, c:c + C_HD], ng) for c in range(0, C_HEADS * C_HD, C_HD)]
    g = gate_ref[...]
    x = jnp.concatenate(parts, axis=1) * (g * jax.nn.sigmoid(g))
    _out_epilogue(_dot(x.astype(BF16), w_ref[...]), *rest)


def _outproj(kind, ins, in_specs, h_all, g1, nf, sc2, sh2, wrh, wrl, br, n_lat):
    t_all, d = h_all.shape
    tm = ROW_TILE
    mod = _mod_spec(n_lat // tm)
    row = lambda w: pl.BlockSpec((tm, w), lambda i: (i, 0))
    full = lambda a: pl.BlockSpec(a.shape, lambda i: (0,) * a.ndim)
    nf = nf.reshape(1, d)
    return pl.pallas_call(
        _about_kernel if kind == "ab" else _cout_kernel,
        grid=(t_all // tm,),
        in_specs=in_specs + [row(d), mod, full(nf), mod, mod, full(wrh), full(wrl), full(br)],
        out_specs=[row(d), pl.BlockSpec((tm * SUBLANES, LANES), lambda i: (i, 0)), row(LANES)],
        out_shape=[
            jax.ShapeDtypeStruct((t_all, d), F32),
            jax.ShapeDtypeStruct((t_all * SUBLANES, LANES), F32),
            jax.ShapeDtypeStruct((t_all, LANES), F32),
        ],
        name="outproj_" + kind,
    )(*ins, h_all, g1, nf, sc2, sh2, wrh, wrl, br)


def _cumsum_rows(tri_bf, x):
    hi = x.astype(BF16)
    r1 = x - hi.astype(F32)
    mid = r1.astype(BF16)
    lo = (r1 - mid.astype(F32)).astype(BF16)
    return _dot(tri_bf, hi) + _dot(tri_bf, mid) + _dot(tri_bf, lo)


def _scan_kernel(qf_ref, zf_ref, vf_ref, qb_ref, zb_ref, vb_ref, lb_ref, of_ref, ob_ref, stf_ref, stb_ref):
    _scan_chunk(qf_ref, zf_ref, vf_ref, lb_ref, of_ref, stf_ref, reverse=False)
    _scan_chunk(qb_ref, zb_ref, vb_ref, lb_ref, ob_ref, stb_ref, reverse=True)


def _scan_chunk(q_ref, z_ref, v_ref, lb_ref, o_ref, st_ref, *, reverse):
    c = pl.program_id(1)

    @pl.when(c == 0)
    def _():
        st_ref[...] = jnp.zeros(st_ref.shape, F32)

    L, SB = C_CHUNK, C_SUB
    nsb = L // SB
    lb = lb_ref[...]
    f = lb + (1.0 - lb) * jax.nn.sigmoid(z_ref[...])
    kk = 1.0 - f
    lf = jnp.log(f)
    r_i = lax.broadcasted_iota(jnp.int32, (L, L), 0)
    c_i = lax.broadcasted_iota(jnp.int32, (L, L), 1)
    tri = (c_i >= r_i) if reverse else (c_i <= r_i)
    cum = _cumsum_rows(tri.astype(BF16), lf)
    last_row = 0 if reverse else L - 1
    last = cum[last_row:last_row + 1]
    q = q_ref[...]
    v = v_ref[...]
    qe = (q * jnp.exp(cum)).astype(BF16)
    kdec = (kk * jnp.exp(last - cum)).astype(BF16)
    e_last = jnp.exp(last)
    v_bf = v.astype(BF16)
    ones = jnp.ones((C_HD, C_HD), BF16)
    sub_r = lax.broadcasted_iota(jnp.int32, (SB, C_HD), 0)
    order = list(range(nsb - 1, -1, -1)) if reverse else list(range(nsb))

    outs = []
    for h in range(C_HEADS):
        hs = slice(h * C_HD, (h + 1) * C_HD)
        st = st_ref[h]
        o_h = _dot_nt(qe[:, hs], st.astype(BF16))
        st_ref[h] = st * e_last[:, hs] + _dot_tn(v_bf[:, hs], kdec[:, hs])
        cum_h, q_h, k_h, v_h = cum[:, hs], q[:, hs], kk[:, hs], v[:, hs]
        o_sub = [None] * nsb
        for p, bi in enumerate(order):
            rows = slice(bi * SB, (bi + 1) * SB)
            cum_i, q_i, k_i, v_i = cum_h[rows], q_h[rows], k_h[rows], v_h[rows]
            groups = SB // SUBLANES
            w_rows, spans = [], []
            for s in range(SB):
                g_s = s // SUBLANES
                live = range(0, g_s + 1) if reverse else range(g_s, groups)
                lo, hi = live[0] * SUBLANES, (live[-1] + 1) * SUBLANES
                ok = (sub_r[lo:hi] <= s) if reverse else (sub_r[lo:hi] >= s)
                e = jnp.where(ok, jnp.exp(cum_i[lo:hi] - cum_i[s:s + 1]), 0.0)
                w_rows.append(q_i[lo:hi] * e * k_i[s:s + 1])
                spans.append(live)
            red = _dot(jnp.concatenate(w_rows, axis=0).astype(BF16), ones)
            acc_g = [jnp.zeros((SUBLANES, C_HD), F32) for _ in range(groups)]
            off = 0
            for s in range(SB):
                for g in spans[s]:
                    acc_g[g] = acc_g[g] + red[off:off + SUBLANES] * v_i[s:s + 1]
                    off += SUBLANES
            acc = jnp.concatenate(acc_g, axis=0)
            if p > 0:
                prev = order[p - 1]
                b_row = prev * SB if reverse else prev * SB + SB - 1
                b = cum_h[b_row:b_row + 1]
                if reverse:
                    past = slice((bi + 1) * SB, L)
                else:
                    past = slice(0, bi * SB)
                qi = (q_i * jnp.exp(cum_i - b)).astype(BF16)
                kp = (k_h[past] * jnp.exp(b - cum_h[past])).astype(BF16)
                att = _dot_nt(qi, kp)
                acc = acc + _dot(att.astype(BF16), v_bf[past, hs])
            o_sub[bi] = acc
        outs.append(o_h + jnp.concatenate(o_sub, axis=0))
    o_ref[...] = jnp.concatenate(outs, axis=1)


def _scan(p5, lb, n_lat, n_ctx, batch):
    t_all = p5.shape[0]
    L = C_CHUNK
    w = C_HEADS * C_HD
    nl, nc = n_lat // L, n_ctx // L
    steps = nl + nc

    def row_idx(reverse):
        def idx(b, c):
            if reverse:
                ctx = batch * nl + b * nc + (nc - 1 - c)
                lat = b * nl + (nl - 1 - (c - nc))
            else:
                ctx = batch * nl + b * nc + c
                lat = b * nl + (c - nc)
            return jnp.where(c < nc, ctx, lat)
        return idx

    fwd, bwd = row_idx(False), row_idx(True)
    col = lambda ri, cb: pl.BlockSpec((L, w), lambda b, c: (ri(b, c), cb))
    out = jax.ShapeDtypeStruct((t_all, w), F32)
    state = pltpu.VMEM((C_HEADS, C_HD, C_HD), F32)
    return pl.pallas_call(
        _scan_kernel,
        grid=(batch, steps),
        in_specs=[col(fwd, 0), col(fwd, 1), col(fwd, 3), col(bwd, 0), col(bwd, 2), col(bwd, 3),
                  pl.BlockSpec((1, w), lambda b, c: (0, 0))],
        out_specs=[col(fwd, 0), col(bwd, 0)],
        out_shape=[out, out],
        scratch_shapes=[state, state],
        compiler_params=pltpu.CompilerParams(dimension_semantics=("arbitrary", "arbitrary")),
        name="hgrn_scan",
    )(p5, p5, p5, p5, p5, p5, lb)


def _route_kernel(lg_ref, tri_ref, sel_ref, gate_ref, cnt_ref, tb_ref, carry_ref):
    i = pl.program_id(0)

    @pl.when(i == 0)
    def _():
        carry_ref[...] = jnp.zeros(carry_ref.shape, F32)

    lg = lg_ref[...]
    lane = lax.broadcasted_iota(jnp.int32, lg.shape, 1)
    lane_f = lane.astype(F32)
    x = jnp.where(lane < N_EXPERTS, lg, -jnp.inf)
    vals, idxs, hits = [], [], []
    for _ in range(TOP_K):
        mk = jnp.max(x, axis=-1, keepdims=True)
        ik = jnp.min(jnp.where(x == mk, lane_f, float(LANES)), axis=-1, keepdims=True)
        hit = lane_f == ik
        x = jnp.where(hit, -jnp.inf, x)
        vals.append(mk)
        idxs.append(ik.astype(jnp.int32))
        hits.append(hit)
    member = hits[0] | hits[1] | hits[2] | hits[3]
    member_f = jnp.where(member, 1.0, 0.0)
    start = carry_ref[0:1]
    before = _dot(tri_ref[...], member_f.astype(BF16)) + start
    tile_rows = [start]
    for t0 in range(0, lg.shape[0] - COMBINE_TILE, COMBINE_TILE):
        tile_rows.append(tile_rows[-1] + jnp.sum(member_f[t0:t0 + COMBINE_TILE], axis=0, keepdims=True))
    tile_rows += [jnp.zeros_like(start)] * (MOD_ROWS - len(tile_rows))
    tb_ref[...] = jnp.concatenate(tile_rows, axis=0)
    carry_ref[0:1] = start + jnp.sum(member_f, axis=0, keepdims=True)
    exps = [jnp.exp(v - vals[0]) for v in vals]
    denom = exps[0] + exps[1] + exps[2] + exps[3]
    sel = jnp.zeros(lg.shape, jnp.int32)
    gate = jnp.zeros(lg.shape, F32)
    for k in range(TOP_K):
        rank = jnp.sum(jnp.where(hits[k], before, 0.0), axis=-1, keepdims=True).astype(jnp.int32)
        sel = jnp.where(lane == k, idxs[k], sel)
        sel = jnp.where(lane == TOP_K + k, rank, sel)
        gate = jnp.where(lane == k, exps[k] / denom, gate)
    sel_ref[...] = sel
    gate_ref[...] = gate
    cnt_ref[...] = carry_ref[...]


def _route(logits):
    t = logits.shape[0]
    tm = ROW_TILE
    assert tm // COMBINE_TILE <= MOD_ROWS
    tri = (jnp.arange(tm)[:, None] > jnp.arange(tm)[None, :]).astype(BF16)
    blk = pl.BlockSpec((tm, LANES), lambda i: (i, 0))
    return pl.pallas_call(
        _route_kernel,
        grid=(t // tm,),
        in_specs=[blk, pl.BlockSpec((tm, tm), lambda i: (0, 0))],
        out_specs=[blk, blk, pl.BlockSpec((MOD_ROWS, LANES), lambda i: (0, 0)),
                   pl.BlockSpec((None, MOD_ROWS, LANES), lambda i: (i, 0, 0))],
        out_shape=[
            jax.ShapeDtypeStruct((t, LANES), jnp.int32),
            jax.ShapeDtypeStruct((t, LANES), F32),
            jax.ShapeDtypeStruct((MOD_ROWS, LANES), F32),
            jax.ShapeDtypeStruct((t // tm, MOD_ROWS, LANES), F32),
        ],
        scratch_shapes=[pltpu.VMEM((MOD_ROWS, LANES), F32)],
        compiler_params=pltpu.CompilerParams(dimension_semantics=("arbitrary",)),
        name="route",
    )(logits, tri)


def _dispatch_kernel(zs_ref, v_ref, tab_hbm, x_hbm, tab_smem, slab, zbuf, sems):
    i = pl.program_id(0)
    tm = v_ref.shape[0] // SUBLANES
    pad = zbuf.shape[0]

    @pl.when(i == 0)
    def _():
        zbuf[...] = jnp.zeros(zbuf.shape, F32)
        for e in range(N_EXPERTS):
            z0 = pl.multiple_of(zs_ref[e] * SUBLANES, SUBLANES)
            pltpu.make_async_copy(zbuf, x_hbm.at[pl.ds(z0, pad), :], sems.at[2]).start()
        for e in range(N_EXPERTS):
            pltpu.make_async_copy(zbuf, x_hbm.at[pl.ds(0, pad), :], sems.at[2]).wait()

    _fetch_table(i, tab_hbm, tab_smem, sems.at[0])

    def place(r, carry):
        row = v_ref[pl.ds(pl.multiple_of(r * SUBLANES, SUBLANES), SUBLANES), :]
        for k in range(TOP_K):
            dst = pl.multiple_of(tab_smem[r * TOP_K + k] * SUBLANES, SUBLANES)
            slab[pl.ds(dst, SUBLANES), :] = row
        return carry

    lax.fori_loop(0, tm, place, 0, unroll=4)
    _run_copies(tab_smem, x_hbm, slab, sems.at[1], to_hbm=True, wait=False)
    _run_copies(tab_smem, x_hbm, slab, sems.at[1], to_hbm=True, wait=True)


def _dispatch(v, table, zero_start, n_rows):
    t = v.shape[0] // SUBLANES
    tm = COMBINE_TILE
    grid_spec = pltpu.PrefetchScalarGridSpec(
        num_scalar_prefetch=1,
        grid=(t // tm,),
        in_specs=[pl.BlockSpec((tm * SUBLANES, LANES), lambda i, zs: (i, 0)), pl.BlockSpec(memory_space=pl.ANY)],
        out_specs=pl.BlockSpec(memory_space=pl.ANY),
        scratch_shapes=[pltpu.SMEM((RUN_TABLE,), jnp.int32), pltpu.VMEM((TOP_K * tm * SUBLANES, LANES), F32),
                        pltpu.VMEM((MOE_TILE * SUBLANES, LANES), F32), pltpu.SemaphoreType.DMA((3,))],
    )
    return pl.pallas_call(
        _dispatch_kernel,
        grid_spec=grid_spec,
        out_shape=jax.ShapeDtypeStruct(((n_rows + MOE_TILE) * SUBLANES, LANES), F32),
        compiler_params=pltpu.CompilerParams(dimension_semantics=("arbitrary",)),
        name="dispatch",
    )(zero_start, v, table)


def _moe_kernel(be_ref, nu_ref, first_ref, x_ref, w1_ref, b1g_ref, b1l_ref, w2_ref, b2_ref, perm_ref, o_ref,
                w1g_scr, w1l_scr, w2_scr):
    i = pl.program_id(0)
    live = i < nu_ref[0]

    @pl.when(jnp.logical_and(live, first_ref[i] == 1))
    def _():
        perm = perm_ref[...]
        group = 2 * LANES
        for b in range(w1_ref.shape[1] // group):
            r = _dot(w1_ref[:, b * group:(b + 1) * group].astype(BF16), perm)
            w1g_scr[:, b * LANES:(b + 1) * LANES] = r[:, :LANES].astype(BF16)
            w1l_scr[:, b * LANES:(b + 1) * LANES] = r[:, LANES:].astype(BF16)
        w2_scr[...] = w2_ref[...].astype(BF16)

    @pl.when(live)
    def _():
        x = _unpack_rows(x_ref, MOE_TILE).astype(BF16)
        hg = _dot(x, w1g_scr[...]) + b1g_ref[...]
        hl = _dot(x, w1l_scr[...]) + b1l_ref[...]
        glu = jnp.minimum(hg, SWIGLU_LIMIT)
        lin = jnp.clip(hl, -SWIGLU_LIMIT, SWIGLU_LIMIT)
        y = glu * jax.nn.sigmoid(SWIGLU_ALPHA * glu) * (lin + 1.0)
        _pack_rows(o_ref, _dot(y.astype(BF16), w2_scr[...]) + b2_ref[...])

    @pl.when(jnp.logical_not(live))
    def _():
        o_ref[...] = jnp.zeros(o_ref.shape, F32)


def _deinterleave_perm():
    src = jnp.arange(2 * LANES)
    dst = jnp.where(src % 2 == 0, src // 2, LANES + src // 2)
    return (dst[:, None] == jnp.arange(2 * LANES)[None, :]).astype(BF16)


def _moe_experts(xb, n_rows, block_expert, n_used, first, layer, w1, b1g, b1l, w2, b2):
    d = w1.shape[2]
    tm = MOE_TILE
    dff = w2.shape[2]
    perm = _deinterleave_perm()
    tile = pl.BlockSpec((tm * SUBLANES, LANES), lambda i, be, nu, fi: (i, 0))
    wspec = lambda k, n: pl.BlockSpec((None, k, n), lambda i, be, nu, fi: (be[i], 0, 0))
    lwspec = lambda k, n: pl.BlockSpec((None, None, k, n), lambda i, be, nu, fi: (layer, be[i], 0, 0))
    grid_spec = pltpu.PrefetchScalarGridSpec(
        num_scalar_prefetch=3,
        grid=(n_rows // tm,),
        in_specs=[
            tile,
            lwspec(d, 2 * dff), wspec(1, dff), wspec(1, dff), lwspec(dff, d), wspec(1, d),
            pl.BlockSpec(perm.shape, lambda i, be, nu, fi: (0, 0)),
        ],
        out_specs=tile,
        scratch_shapes=[pltpu.VMEM((d, dff), BF16), pltpu.VMEM((d, dff), BF16), pltpu.VMEM((dff, d), BF16)],
    )
    return pl.pallas_call(
        _moe_kernel,
        grid_spec=grid_spec,
        out_shape=jax.ShapeDtypeStruct((n_rows * SUBLANES, LANES), F32),
        compiler_params=pltpu.CompilerParams(dimension_semantics=("arbitrary",), vmem_limit_bytes=MOE_VMEM_BYTES),
        name="moe_experts",
    )(block_expert, n_used, first, xb, w1, b1g, b1l, w2, b2, perm)


def _moe(v, logits, layer, w1, b1g, b1l, w2, b2):
    t = logits.shape[0]
    tm = MOE_TILE
    sel, gates, cnt, tile_cnt = _route(logits)
    counts = cnt[0, :N_EXPERTS].astype(jnp.int32)
    padded = (counts + tm - 1) // tm * tm
    pad_end = jnp.cumsum(padded)
    pad_start = pad_end - padded
    n_blocks = t * TOP_K // tm + N_EXPERTS
    n_rows = n_blocks * tm
    block_start = jnp.arange(n_blocks, dtype=jnp.int32) * tm
    block_expert = jnp.minimum(jnp.sum(block_start[:, None] >= pad_end[None, :], axis=1), N_EXPERTS - 1)
    block_expert = block_expert.astype(jnp.int32)
    n_used = (pad_end[-1] // tm).astype(jnp.int32).reshape(1)
    first = jnp.concatenate([jnp.ones((1,), jnp.int32), (block_expert[1:] != block_expert[:-1]).astype(jnp.int32)])
    n_tiles = t // COMBINE_TILE
    per_block = ROW_TILE // COMBINE_TILE
    before = tile_cnt[:, :per_block, :N_EXPERTS].astype(jnp.int32).reshape(n_tiles, N_EXPERTS)
    run_len = jnp.concatenate([before[1:], counts[None, :]], axis=0) - before
    slab_off = jnp.cumsum(run_len, axis=1) - run_len
    expert = sel[:, :TOP_K].reshape(n_tiles, COMBINE_TILE * TOP_K)
    rank = sel[:, TOP_K:2 * TOP_K].reshape(n_tiles, COMBINE_TILE * TOP_K)
    hit = expert[:, :, None] == jnp.arange(N_EXPERTS)[None, None, :]
    slab_row = rank + jnp.sum(jnp.where(hit, (slab_off - before)[:, None, :], 0), axis=-1)
    fill = jnp.zeros((n_tiles, RUN_TABLE - RUN_LEN - N_EXPERTS), jnp.int32)
    table = jnp.concatenate([slab_row, pad_start[None, :] + before, run_len, fill], axis=1).astype(jnp.int32)
    x_sorted = _dispatch(v, table, (pad_start + counts).astype(jnp.int32), n_rows)
    y_sorted = _moe_experts(x_sorted, n_rows, block_expert, n_used, first, layer, w1, b1g, b1l, w2, b2)
    return gates, table, y_sorted


def _final_kernel(h_ref, g2_ref, gate_ref, tab_hbm, y_hbm, g_ref, o_ref, *combine_scratch):
    moe = _moe_combine(pl.program_id(0), tab_hbm, y_hbm, gate_ref, *combine_scratch)
    o_ref[...] = _rms(h_ref[...] + g2_ref[...] * moe, g_ref[...])


def _final_norm(h_all, pending, g, t_lat, n_lat):
    d = h_all.shape[1]
    tm = COMBINE_TILE
    g2, gates, dest, y_sorted = pending
    row = pl.BlockSpec((tm, d), lambda i: (i, 0))
    any_spec = pl.BlockSpec(memory_space=pl.ANY)
    return pl.pallas_call(
        _final_kernel,
        grid=(t_lat // tm,),
        in_specs=[row, _mod_spec(n_lat // tm), pl.BlockSpec((tm, LANES), lambda i: (i, 0)), any_spec, any_spec,
                  pl.BlockSpec((1, d), lambda i: (0, 0))],
        out_specs=row,
        out_shape=jax.ShapeDtypeStruct((t_lat, d), F32),
        scratch_shapes=_combine_scratch(tm, d),
        compiler_params=pltpu.CompilerParams(dimension_semantics=("arbitrary",)),
        name="final_norm",
    )(h_all, g2, gates, dest, y_sorted, g.reshape(1, d))


def _rope_tables(n_lat, n_ctx_pad):
    quarter = A_HD // 4
    inv_freq = ROPE_THETA ** (-jnp.arange(quarter, dtype=F32) / quarter)
    t = jnp.arange(n_lat)
    row = (t // GRID_W).astype(F32)
    colp = (t % GRID_W).astype(F32)
    ang_r = row[:, None] * inv_freq
    ang_c = colp[:, None] * inv_freq
    cos64 = jnp.concatenate([jnp.cos(ang_r), jnp.cos(ang_r), jnp.cos(ang_c), jnp.cos(ang_c)], axis=1)
    sin64 = jnp.concatenate([-jnp.sin(ang_r), jnp.sin(ang_r), -jnp.sin(ang_c), jnp.sin(ang_c)], axis=1)
    one, zero = jnp.ones_like(cos64), jnp.zeros_like(sin64)

    def pad(a, fill):
        return jnp.concatenate([a, jnp.full((n_ctx_pad, a.shape[1]), fill, F32)], axis=0)

    cos_a = pad(jnp.concatenate([cos64, cos64], axis=1), 1.0)
    sin_a = pad(jnp.concatenate([sin64, sin64], axis=1), 0.0)
    cos_b = pad(jnp.concatenate([cos64, one], axis=1), 1.0)
    sin_b = pad(jnp.concatenate([sin64, zero], axis=1), 0.0)
    return cos_a, sin_a, cos_b, sin_b


def _prep_uq(w_uq):
    r = w_uq.shape[0]
    w = w_uq.reshape(r, B_HEADS, B_NOPE + B_ROPE)
    w = jnp.pad(w, ((0, 0), (0, 0), (0, 2 * LANES - B_NOPE - B_ROPE)))
    return w.reshape(r, B_HEADS * 2 * LANES).astype(BF16)


def _prep_ukv(w_ukv):
    r = w_ukv.shape[0]
    w = w_ukv.reshape(r, B_HEADS, B_NOPE + B_VD)
    wk = w[:, :, :B_NOPE].reshape(r, B_HEADS * B_NOPE)
    wv = w[:, :, B_NOPE:].reshape(r, B_HEADS * B_VD)
    return wk.astype(BF16), wv.astype(BF16)


def kernel(x, c, ctx, c_ctx, norm_mix_g, norm_ffn_g, w_ada, b_ada, w_in_ab, diff_lambda, diff_subln_g, mla_q_norm_g, mla_kv_norm_g, w_uq, w_ukv, w_out_ab, w_in_c, lb_raw, hgrn_norm_g, w_out_c, w_router, b_router, w_exp1, b_exp1, w_exp2, b_exp2, final_g):
    batch, n_lat, d = x.shape
    n_ctx = ctx.shape[1]
    t_lat = batch * n_lat
    h_all = jnp.concatenate([x.reshape(t_lat, d), ctx.reshape(batch * n_ctx, d)], axis=0)

    cvec = jnp.concatenate([c, c_ctx[None, :], jnp.zeros((MOD_ROWS - batch - 1, d), F32)], axis=0)
    mods = _ada(cvec, w_ada, b_ada)

    lb_p = jax.nn.softmax(lb_raw.astype(F32), axis=0)
    lower_bounds = jnp.cumsum(lb_p, axis=0) - lb_p[0]
    tabs = _rope_tables(n_lat, PREP_TILE)

    kv_len = n_lat + n_ctx
    tq = min(1024, n_lat)
    tk = next(t for t in (768, 512, 256) if kv_len % t == 0)

    pending = None
    for l in range(DEPTH):
        j = l // 2
        m = mods[l]
        sh1, sc1, g1, sh2, sc2, g2 = [m[:, k * d:(k + 1) * d].reshape(MOD_ROWS, 1, d) for k in range(6)]
        wr = jnp.pad(w_router[l], ((0, 0), (0, LANES - N_EXPERTS)))
        wrh, wrl = _split_hi_lo(wr)
        br = jnp.pad(b_router[l].astype(F32), (0, LANES - N_EXPERTS)).reshape(1, LANES)
        row = lambda w: pl.BlockSpec((ROW_TILE, w), lambda i: (i, 0))
        full = lambda a: pl.BlockSpec(a.shape, lambda i: (0,) * a.ndim)
        if l % 2 == 0:
            w_in, tn = jnp.pad(w_in_ab[j], ((0, 0), (0, AB_PROJ_PAD - w_in_ab.shape[2]))).astype(BF16), AB_PROJ_PAD
        else:
            w_in, tn = w_in_c[j].astype(BF16), 1024
        p, h_all = _inproj(h_all, norm_mix_g[l], sc1, sh1, w_in, n_lat, tn, pending)
        tail = (h_all, g1, norm_ffn_g[l], sc2, sh2, wrh, wrl, br, n_lat)
        if l % 2 == 0:
            lam_init = 0.8 - 0.6 * math.exp(-0.3 * l)
            wuk, wuv = _prep_ukv(w_ukv[j])
            qa, qb, ka, va, kb, vb = _abprep(
                p, tabs, mla_q_norm_g[j].reshape(1, -1), mla_kv_norm_g[j].reshape(1, -1), _prep_uq(w_uq[j]), wuk, wuv,
                n_lat, n_ctx, batch)
            extra = (diff_lambda[j].astype(F32), diff_subln_g[j].reshape(1, A_VD))
            lat = dict(batch=batch, tq=tq, tk=tk, q_blk0=0, q_blk_stride=n_lat // tq, nq=n_lat // tq,
                       kv_blk_stride=kv_len // tk, nkv=kv_len // tk)
            oa = _flash("diff", qa, ka, va, extra, heads=A_HEADS, lam_init=lam_init, **lat)
            ob = _flash("mla", qb, kb, vb, (), heads=B_HEADS, **lat)
            if l != DEPTH - 1:
                cq = dict(batch=batch, tq=n_ctx, tk=n_ctx, q_blk0=t_lat // n_ctx, q_blk_stride=1, nq=1,
                          kv_blk_stride=kv_len // n_ctx, nkv=1)
                oa = _flash("diff", qa, ka, va, extra, heads=A_HEADS, lam_init=lam_init, out_init=oa, **cq)
                ob = _flash("mla", qb, kb, vb, (), heads=B_HEADS, out_init=ob, **cq)
            wo = w_out_ab[j].astype(BF16)
            wa, wb = wo[:A_HEADS * A_VD], wo[A_HEADS * A_VD:]
            h_mid, v_ffn, logits = _outproj(
                "ab", (oa, ob, wa, wb), [row(oa.shape[1]), row(ob.shape[1]), full(wa), full(wb)], *tail)
        else:
            p5 = p
            lb = lower_bounds[l].reshape(1, -1)
            o_f, o_b = _scan(p5, lb, n_lat, n_ctx, batch)
            ng = hgrn_norm_g[j].reshape(1, C_HD)
            wo = w_out_c[j].astype(BF16)
            gate_spec = pl.BlockSpec((ROW_TILE, d), lambda i: (i, 4))
            h_mid, v_ffn, logits = _outproj(
                "c", (o_f, o_b, p5, ng, wo), [row(d), row(d), gate_spec, full(ng), full(wo)], *tail)

        b1 = b_exp1[l].astype(F32)
        b1g, b1l = b1[:, None, 0::2], b1[:, None, 1::2]
        pending = (g2,) + _moe(v_ffn, logits, l, w_exp1, b1g, b1l, w_exp2, b_exp2[l].astype(F32)[:, None, :])
        h_all = h_mid

    return _final_norm(h_all, pending, final_g, t_lat, n_lat).reshape(batch, n_lat, d)
```

```python
import functools
import math

import jax
import jax.numpy as jnp
from jax import lax
from jax.experimental import pallas as pl
from jax.experimental.pallas import tpu as pltpu

F32 = jnp.float32
BF16 = jnp.bfloat16

D_MODEL = 1024
DEPTH = 4
GRID_W = 64
ROPE_THETA = 10000.0
NORM_EPS = 1e-6

A_HEADS = 4
A_HD = 64
A_VD = 128
A_SCALE = A_HD ** -0.5
B_HEADS = 4
B_NOPE = 128
B_ROPE = 64
B_VD = 128
B_Q_RANK = 256
B_KV_RANK = 128
B_SCALE = (B_NOPE + B_ROPE) ** -0.5
LOG2E = math.log2(math.e)
AB_PROJ_PAD = 2048

C_HEADS = 8
C_HD = 128
C_CHUNK = 64
C_SUB = 16

N_EXPERTS = 32
TOP_K = 4
SWIGLU_ALPHA = 1.702
SWIGLU_LIMIT = 7.0

LANES = 128
SUBLANES = 8
ROW_TILE = 512
PREP_TILE = 256
FLASH_SUB = 256
FLASH_TQ = 2048
FLASH_TK_CHOICES = (2816, 768, 512, 256)
MOE_TILE = 256
COMBINE_TILE = 256
MOD_ROWS = 8
MOE_VMEM_BYTES = 52 * 1024 * 1024


def _split_hi_lo(x):
    hi = x.astype(BF16)
    lo = (x - hi.astype(F32)).astype(BF16)
    return hi, lo


def _dot(a, b):
    return jnp.dot(a, b, preferred_element_type=F32)


def _dot_nt(a, b):
    return lax.dot_general(a, b, (((1,), (1,)), ((), ())), preferred_element_type=F32)


def _dot_tn(a, b):
    return lax.dot_general(a, b, (((0,), (0,)), ((), ())), preferred_element_type=F32)


def _rms(x, g):
    ms = jnp.mean(x * x, axis=-1, keepdims=True)
    return x * lax.rsqrt(ms + NORM_EPS) * g


def _ada_kernel(c_ref, w_ref, b_ref, o_ref):
    c = c_ref[...]
    s = c * jax.nn.sigmoid(c)
    s_hi, s_lo = _split_hi_lo(s)
    w_hi, w_lo = _split_hi_lo(w_ref[...])
    o_ref[...] = _dot(s_hi, w_hi) + _dot(s_lo, w_hi) + _dot(s_hi, w_lo) + b_ref[...]


def _ada(cvec, w_ada, b_ada):
    depth, d, n6 = w_ada.shape
    tn = 1536
    return pl.pallas_call(
        _ada_kernel,
        grid=(depth, n6 // tn),
        in_specs=[
            pl.BlockSpec((MOD_ROWS, d), lambda l, j: (0, 0)),
            pl.BlockSpec((None, d, tn), lambda l, j: (l, 0, j)),
            pl.BlockSpec((None, 1, tn), lambda l, j: (l, 0, j)),
        ],
        out_specs=pl.BlockSpec((None, MOD_ROWS, tn), lambda l, j: (l, 0, j)),
        out_shape=jax.ShapeDtypeStruct((depth, MOD_ROWS, n6), F32),
        name="ada_mod",
    )(cvec, w_ada, b_ada.reshape(depth, 1, n6))


def _unpack_rows(ref, tm):
    return jnp.concatenate([ref[pl.ds(s, tm, stride=SUBLANES), :] for s in range(SUBLANES)], axis=1)


def _pack_rows(ref, x):
    tm = x.shape[0]
    for s in range(SUBLANES):
        ref[pl.ds(s, tm, stride=SUBLANES), :] = x[:, s * LANES:(s + 1) * LANES]


RUN_STRIDE = 2 * N_EXPERTS
RUN_BITS = tuple(1 << b for b in range(COMBINE_TILE.bit_length() - 1, -1, -1))


def _run_copies(runs_ref, tile, hbm, slab, sem, *, to_hbm, wait):
    base = tile * RUN_STRIDE

    def per_expert(e, cur):
        row0 = runs_ref[base + e]
        n = runs_ref[base + N_EXPERTS + e]
        pos = jnp.int32(0)
        for bit in RUN_BITS:
            take = (n & bit) != 0

            @pl.when(take)
            def _():
                h = hbm.at[pl.ds(pl.multiple_of((row0 + pos) * SUBLANES, SUBLANES), bit * SUBLANES), :]
                s = slab.at[pl.ds(pl.multiple_of((cur + pos) * SUBLANES, SUBLANES), bit * SUBLANES), :]
                cp = pltpu.make_async_copy(s, h, sem) if to_hbm else pltpu.make_async_copy(h, s, sem)
                if wait:
                    cp.wait()
                else:
                    cp.start()

            pos = pos + jnp.where(take, bit, 0)
        return cur + n

    lax.fori_loop(0, N_EXPERTS, per_expert, jnp.int32(0))


def _moe_combine(runs_ref, rows_hbm, y_hbm, gate_ref, rows_smem, slabs, ybuf, sems):
    i = pl.program_id(0)
    slot = i % 2
    tm = ybuf.shape[1] // SUBLANES
    rows_cp = pltpu.make_async_copy(rows_hbm.at[i], rows_smem, sems.at[0])
    rows_cp.start()
    fetch = functools.partial(_run_copies, runs_ref, hbm=y_hbm, to_hbm=False)

    @pl.when(i == 0)
    def _():
        fetch(0, slab=slabs.at[0], sem=sems.at[1], wait=False)

    @pl.when(i + 1 < pl.num_programs(0))
    def _():
        fetch(i + 1, slab=slabs.at[1 - slot], sem=sems.at[2 - slot], wait=False)

    fetch(i, slab=slabs.at[slot], sem=sems.at[1 + slot], wait=True)
    rows_cp.wait()
    slab = slabs.at[slot]

    def place(r, carry):
        for k in range(TOP_K):
            src = pl.multiple_of(rows_smem[r * TOP_K + k] * SUBLANES, SUBLANES)
            ybuf[k, pl.ds(pl.multiple_of(r * SUBLANES, SUBLANES), SUBLANES), :] = slab[pl.ds(src, SUBLANES), :]
        return carry

    lax.fori_loop(0, tm, place, 0, unroll=4)
    gates = gate_ref[...]
    acc = gates[:, 0:1] * _unpack_rows(ybuf.at[0], tm)
    for k in range(1, TOP_K):
        acc = acc + gates[:, k:k + 1] * _unpack_rows(ybuf.at[k], tm)
    return acc


def _inproj_kernel(h_ref, g_ref, sc_ref, sh_ref, w_ref, o_ref, u_scr):
    @pl.when(pl.program_id(1) == 0)
    def _():
        u = _rms(h_ref[...], g_ref[...]) * (1.0 + sc_ref[...]) + sh_ref[...]
        u_scr[...] = u.astype(BF16)

    o_ref[...] = _dot(u_scr[...], w_ref[...])


def _inproj_moe_kernel(runs_ref, h_ref, g2_ref, gate_ref, rows_hbm, y_hbm, g_ref, sc_ref, sh_ref, w_ref, o_ref,
                       hn_ref, u_scr, *combine_scratch):
    @pl.when(pl.program_id(1) == 0)
    def _():
        moe = _moe_combine(runs_ref, rows_hbm, y_hbm, gate_ref, *combine_scratch)
        h = h_ref[...] + g2_ref[...] * moe
        hn_ref[...] = h
        u = _rms(h, g_ref[...]) * (1.0 + sc_ref[...]) + sh_ref[...]
        u_scr[...] = u.astype(BF16)

    o_ref[...] = _dot(u_scr[...], w_ref[...])


def _mod_spec(n_lat_blocks):
    def idx(i, *_):
        return (jnp.minimum(i // n_lat_blocks, 2), 0, 0)

    return pl.BlockSpec((None, 1, D_MODEL), idx)


def _combine_scratch(tm, d):
    assert d == SUBLANES * LANES and tm == COMBINE_TILE
    return [pltpu.SMEM((tm * TOP_K,), jnp.int32), pltpu.VMEM((2, TOP_K * tm * SUBLANES, LANES), F32),
            pltpu.VMEM((TOP_K, tm * SUBLANES, LANES), F32), pltpu.SemaphoreType.DMA((3,))]


def _inproj(h_all, gain, sc, sh, w_bf, n_lat, tn, pending=None):
    t_all, d = h_all.shape
    nout = w_bf.shape[1]
    tm = COMBINE_TILE
    mod = _mod_spec(n_lat // tm)
    row = pl.BlockSpec((tm, d), lambda i, j, *_: (i, 0))
    tail_specs = [pl.BlockSpec((1, d), lambda i, j, *_: (0, 0)), mod, mod,
                  pl.BlockSpec((d, tn), lambda i, j, *_: (0, j))]
    tail = (gain.reshape(1, d), sc, sh, w_bf)
    p_spec = pl.BlockSpec((tm, tn), lambda i, j, *_: (i, j))
    p_shape = jax.ShapeDtypeStruct((t_all, nout), F32)
    params = pltpu.CompilerParams(dimension_semantics=("arbitrary", "arbitrary"))
    if pending is None:
        p = pl.pallas_call(
            _inproj_kernel,
            grid=(t_all // tm, nout // tn),
            in_specs=[row] + tail_specs,
            out_specs=p_spec,
            out_shape=p_shape,
            scratch_shapes=[pltpu.VMEM((tm, d), BF16)],
            compiler_params=params,
            name="inproj",
        )(h_all, *tail)
        return p, h_all
    g2, gates, runs, rows, y_sorted = pending
    any_spec = pl.BlockSpec(memory_space=pl.ANY)
    grid_spec = pltpu.PrefetchScalarGridSpec(
        num_scalar_prefetch=1,
        grid=(t_all // tm, nout // tn),
        in_specs=[row, mod, pl.BlockSpec((tm, LANES), lambda i, j, *_: (i, 0)), any_spec, any_spec] + tail_specs,
        out_specs=[p_spec, row],
        scratch_shapes=[pltpu.VMEM((tm, d), BF16)] + _combine_scratch(tm, d),
    )
    return pl.pallas_call(
        _inproj_moe_kernel,
        grid_spec=grid_spec,
        out_shape=[p_shape, jax.ShapeDtypeStruct((t_all, d), F32)],
        compiler_params=params,
        name="inproj_moe",
    )(runs, h_all, g2, gates, rows, y_sorted, *tail)


def _rope(x, cos, sin):
    n = x.shape[-1]
    lane = lax.broadcasted_iota(jnp.int32, x.shape, 1)
    first = (lane // 16) % 2 == 0
    partner = jnp.where(first, pltpu.roll(x, n - 16, 1), pltpu.roll(x, 16, 1))
    return x * cos + partner * sin


def _abprep_kernel(p_ref, cosa_ref, sina_ref, cosb_ref, sinb_ref, qg_ref, kvg_ref, wuq_ref, wuk_ref, wuv_ref,
                   qa_ref, qb_ref, ka_ref, va_ref, kb_ref, vb_ref):
    cosa, sina = cosa_ref[...], sina_ref[...]
    cosb, sinb = cosb_ref[...], sinb_ref[...]
    n_a = 2 * A_HEADS * A_HD
    qa = [_rope(p_ref[:, c:c + LANES], cosa, sina) * (A_SCALE * LOG2E) for c in range(0, n_a, LANES)]
    qa_ref[...] = jnp.concatenate(qa, axis=1).astype(BF16)
    ka = [_rope(p_ref[:, n_a + c:n_a + c + LANES], cosa, sina) for c in range(0, n_a, LANES)]
    ka_ref[...] = jnp.concatenate(ka, axis=1).astype(BF16)
    va_ref[...] = p_ref[:, 2 * n_a:2 * n_a + A_HEADS * A_VD].astype(BF16)

    off = 2 * n_a + A_HEADS * A_VD
    cq = _rms(p_ref[:, off:off + B_Q_RANK], qg_ref[...]).astype(BF16)
    qf = _dot(cq, wuq_ref[...])
    ckv = _rms(p_ref[:, off + B_Q_RANK:off + B_Q_RANK + B_KV_RANK], kvg_ref[...]).astype(BF16)
    kn = _dot(ckv, wuk_ref[...])
    vb_ref[...] = _dot(ckv, wuv_ref[...]).astype(BF16)
    kr_off = off + B_Q_RANK + B_KV_RANK
    krr = _rope(p_ref[:, kr_off:kr_off + LANES], cosb, sinb)
    qb, kb = [], []
    for h in range(B_HEADS):
        qb.append(qf[:, 2 * LANES * h:2 * LANES * h + LANES] * (B_SCALE * LOG2E))
        qb.append(_rope(qf[:, 2 * LANES * h + LANES:2 * LANES * (h + 1)], cosb, sinb) * (B_SCALE * LOG2E))
        kb.append(kn[:, LANES * h:LANES * (h + 1)])
        kb.append(krr)
    qb_ref[...] = jnp.concatenate(qb, axis=1).astype(BF16)
    kb_ref[...] = jnp.concatenate(kb, axis=1).astype(BF16)


def _abprep(p, tabs, qg, kvg, wuq, wuk, wuv, n_lat, n_ctx, batch):
    t_all = p.shape[0]
    tm = PREP_TILE
    nlb = n_lat // tm
    ncb = n_ctx // tm
    kvb = nlb + ncb

    def tab_idx(i):
        return (jnp.where(i < batch * nlb, i % nlb, nlb), 0)

    def kv_idx(i):
        lat = (i // nlb) * kvb + ncb + i % nlb
        j = i - batch * nlb
        ctx = (j // ncb) * kvb + j % ncb
        return (jnp.where(i < batch * nlb, lat, ctx), 0)

    tab_spec = pl.BlockSpec((tm, LANES), tab_idx)
    full = lambda a: pl.BlockSpec(a.shape, lambda i: (0,) * a.ndim)
    wq, wk, wv = 2 * A_HEADS * A_HD, B_HEADS * 2 * LANES, A_HEADS * A_VD
    kv_rows = batch * (n_lat + n_ctx)
    return pl.pallas_call(
        _abprep_kernel,
        grid=(t_all // tm,),
        in_specs=[pl.BlockSpec((tm, AB_PROJ_PAD), lambda i: (i, 0)), tab_spec, tab_spec, tab_spec, tab_spec,
                  full(qg), full(kvg), full(wuq), full(wuk), full(wuv)],
        out_specs=[
            pl.BlockSpec((tm, wq), lambda i: (i, 0)),
            pl.BlockSpec((tm, wk), lambda i: (i, 0)),
            pl.BlockSpec((tm, wq), kv_idx),
            pl.BlockSpec((tm, wv), kv_idx),
            pl.BlockSpec((tm, wk), kv_idx),
            pl.BlockSpec((tm, wv), kv_idx),
        ],
        out_shape=[
            jax.ShapeDtypeStruct((t_all, wq), BF16),
            jax.ShapeDtypeStruct((t_all, wk), BF16),
            jax.ShapeDtypeStruct((kv_rows, wq), BF16),
            jax.ShapeDtypeStruct((kv_rows, wv), BF16),
            jax.ShapeDtypeStruct((kv_rows, wk), BF16),
            jax.ShapeDtypeStruct((kv_rows, wv), BF16),
        ],
        name="ab_prep",
    )(p, *tabs, qg, kvg, wuq, wuk, wuv)


def _flash_steps(j, q_maps, k_ref, v_ref, m_refs, acc_refs):
    @pl.when(j == 0)
    def _():
        for m_ref, acc_ref in zip(m_refs, acc_refs):
            m_ref[...] = jnp.full(m_ref.shape, -jnp.inf, F32)
            acc_ref[...] = jnp.zeros(acc_ref.shape, F32)

    k = k_ref[...]
    v = v_ref[...]
    v_ext = jnp.concatenate([v, jnp.ones(v.shape, BF16)], axis=1)
    tq = m_refs[0].shape[0]
    sub = min(FLASH_SUB, tq)
    for r0 in range(0, tq, sub):
        rows = slice(r0, r0 + sub)
        for q_of, m_ref, acc_ref in zip(q_maps, m_refs, acc_refs):
            s = _dot_nt(q_of(rows), k)
            m_prev = m_ref[rows, :]
            m_new = jnp.maximum(m_prev, jnp.max(s, axis=-1, keepdims=True))
            p = jnp.exp2(s - m_new).astype(BF16)
            acc_ref[rows, :] = jnp.exp2(m_prev - m_new) * acc_ref[rows, :] + _dot(p, v_ext)
            m_ref[rows, :] = m_new


def _flash_result(acc_ref):
    wv = acc_ref.shape[1] // 2
    return acc_ref[:, :wv] / acc_ref[:, wv:]


def _flash_diff_kernel(q_ref, k_ref, v_ref, lam_ref, g_ref, o_ref, m1_ref, m2_ref, acc1_ref, acc2_ref, *, lam_init):
    j = pl.program_id(3)

    def q_half(upper):
        def get(rows):
            q = q_ref[rows, :]
            lane = lax.broadcasted_iota(jnp.int32, q.shape, 1)
            return jnp.where((lane >= A_HD) == upper, q, jnp.zeros_like(q))
        return get

    _flash_steps(j, (q_half(False), q_half(True)), k_ref, v_ref, (m1_ref, m2_ref), (acc1_ref, acc2_ref))

    @pl.when(j == pl.num_programs(3) - 1)
    def _():
        lf = lam_ref[...]
        lam = (jnp.exp(jnp.sum(lf[0:1] * lf[1:2], axis=-1, keepdims=True))
               - jnp.exp(jnp.sum(lf[2:3] * lf[3:4], axis=-1, keepdims=True)) + lam_init)
        o = _flash_result(acc1_ref) - lam * _flash_result(acc2_ref)
        o_ref[...] = (_rms(o, g_ref[...]) * (1.0 - lam_init)).astype(o_ref.dtype)


def _flash_mla_kernel(q_ref, k_ref, v_ref, o_ref, m_ref, acc_ref):
    j = pl.program_id(3)
    _flash_steps(j, (lambda rows: q_ref[rows, :],), k_ref, v_ref, (m_ref,), (acc_ref,))

    @pl.when(j == pl.num_programs(3) - 1)
    def _():
        o_ref[...] = _flash_result(acc_ref).astype(o_ref.dtype)


def _flash(kind, q, k, v, extra, *, batch, heads, tq, tk, q_blk0, q_blk_stride, nq, kv_blk_stride, nkv, lam_init=0.0,
           out_init=None):
    wq = q.shape[1] // heads
    wv = v.shape[1] // heads
    n_maps = 2 if kind == "diff" else 1
    q_spec = pl.BlockSpec((tq, wq), lambda b, h, i, j: (q_blk0 + b * q_blk_stride + i, h))
    k_spec = pl.BlockSpec((tk, wq), lambda b, h, i, j: (b * kv_blk_stride + j, h))
    v_spec = pl.BlockSpec((tk, wv), lambda b, h, i, j: (b * kv_blk_stride + j, h))
    o_spec = pl.BlockSpec((tq, wv), lambda b, h, i, j: (q_blk0 + b * q_blk_stride + i, h))
    in_specs = [q_spec, k_spec, v_spec]
    if kind == "diff":
        body = functools.partial(_flash_diff_kernel, lam_init=lam_init)
        in_specs += [pl.BlockSpec(e.shape, lambda b, h, i, j: (0, 0)) for e in extra]
    else:
        body = _flash_mla_kernel
    args = [q, k, v, *extra]
    aliases = {}
    if out_init is not None:
        n_in = len(args)
        in_specs.append(pl.BlockSpec(memory_space=pl.ANY))
        args.append(out_init)
        aliases = {n_in: 0}
        inner = body
        body = lambda *refs: inner(*refs[:n_in], *refs[n_in + 1:])
    return pl.pallas_call(
        body,
        grid=(batch, heads, nq, nkv),
        in_specs=in_specs,
        out_specs=o_spec,
        input_output_aliases=aliases,
        out_shape=jax.ShapeDtypeStruct((q.shape[0], heads * wv), BF16),
        scratch_shapes=([pltpu.VMEM((tq, 1), F32) for _ in range(n_maps)]
                        + [pltpu.VMEM((tq, 2 * wv), F32) for _ in range(n_maps)]),
        compiler_params=pltpu.CompilerParams(
            dimension_semantics=("arbitrary", "arbitrary", "arbitrary", "arbitrary")),
        name="flash_" + kind,
    )(*args)


def _out_epilogue(y, h_ref, g1_ref, nf_ref, sc2_ref, sh2_ref, wrh_ref, wrl_ref, br_ref, hn_ref, v_ref, lg_ref):
    hn = h_ref[...] + g1_ref[...] * y
    hn_ref[...] = hn
    v = _rms(hn, nf_ref[...]) * (1.0 + sc2_ref[...]) + sh2_ref[...]
    _pack_rows(v_ref, v)
    v_hi, v_lo = _split_hi_lo(v)
    wrh = wrh_ref[...]
    lg_ref[...] = _dot(v_hi, wrh) + _dot(v_lo, wrh) + _dot(v_hi, wrl_ref[...]) + br_ref[...]


def _about_kernel(oa_ref, ob_ref, wa_ref, wb_ref, *rest):
    y = _dot(oa_ref[...], wa_ref[...]) + _dot(ob_ref[...], wb_ref[...])
    _out_epilogue(y, *rest)


def _cout_kernel(of_ref, obk_ref, gate_ref, ng_ref, w_ref, *rest):
    o = of_ref[...] + obk_ref[...]
    ng = ng_ref[...]
    parts = [_rms(o[:, c:c + C_HD], ng) for c in range(0, C_HEADS * C_HD, C_HD)]
    g = gate_ref[...]
    x = jnp.concatenate(parts, axis=1) * (g * jax.nn.sigmoid(g))
    _out_epilogue(_dot(x.astype(BF16), w_ref[...]), *rest)


def _outproj(kind, ins, in_specs, h_all, g1, nf, sc2, sh2, wrh, wrl, br, n_lat):
    t_all, d = h_all.shape
    tm = ROW_TILE
    mod = _mod_spec(n_lat // tm)
    row = lambda w: pl.BlockSpec((tm, w), lambda i: (i, 0))
    full = lambda a: pl.BlockSpec(a.shape, lambda i: (0,) * a.ndim)
    nf = nf.reshape(1, d)
    return pl.pallas_call(
        _about_kernel if kind == "ab" else _cout_kernel,
        grid=(t_all // tm,),
        in_specs=in_specs + [row(d), mod, full(nf), mod, mod, full(wrh), full(wrl), full(br)],
        out_specs=[row(d), pl.BlockSpec((tm * SUBLANES, LANES), lambda i: (i, 0)), row(LANES)],
        out_shape=[
            jax.ShapeDtypeStruct((t_all, d), F32),
            jax.ShapeDtypeStruct((t_all * SUBLANES, LANES), F32),
            jax.ShapeDtypeStruct((t_all, LANES), F32),
        ],
        name="outproj_" + kind,
    )(*ins, h_all, g1, nf, sc2, sh2, wrh, wrl, br)


def _cumsum_rows(tri_bf, x):
    hi = x.astype(BF16)
    r1 = x - hi.astype(F32)
    mid = r1.astype(BF16)
    lo = (r1 - mid.astype(F32)).astype(BF16)
    return _dot(tri_bf, hi) + _dot(tri_bf, mid) + _dot(tri_bf, lo)


def _scan_kernel(qf_ref, zf_ref, vf_ref, qb_ref, zb_ref, vb_ref, lb_ref, of_ref, ob_ref, stf_ref, stb_ref):
    _scan_chunk(qf_ref, zf_ref, vf_ref, lb_ref, of_ref, stf_ref, reverse=False)
    _scan_chunk(qb_ref, zb_ref, vb_ref, lb_ref, ob_ref, stb_ref, reverse=True)


def _scan_chunk(q_ref, z_ref, v_ref, lb_ref, o_ref, st_ref, *, reverse):
    c = pl.program_id(1)

    @pl.when(c == 0)
    def _():
        st_ref[...] = jnp.zeros(st_ref.shape, F32)

    L, SB = C_CHUNK, C_SUB
    nsb = L // SB
    lb = lb_ref[...]
    f = lb + (1.0 - lb) * jax.nn.sigmoid(z_ref[...])
    kk = 1.0 - f
    lf = jnp.log(f)
    r_i = lax.broadcasted_iota(jnp.int32, (L, L), 0)
    c_i = lax.broadcasted_iota(jnp.int32, (L, L), 1)
    tri = (c_i >= r_i) if reverse else (c_i <= r_i)
    cum = _cumsum_rows(tri.astype(BF16), lf)
    last_row = 0 if reverse else L - 1
    last = cum[last_row:last_row + 1]
    q = q_ref[...]
    v = v_ref[...]
    qe = (q * jnp.exp(cum)).astype(BF16)
    kdec = (kk * jnp.exp(last - cum)).astype(BF16)
    e_last = jnp.exp(last)
    v_bf = v.astype(BF16)
    ones = jnp.ones((C_HD, C_HD), BF16)
    sub_r = lax.broadcasted_iota(jnp.int32, (SB, C_HD), 0)
    order = list(range(nsb - 1, -1, -1)) if reverse else list(range(nsb))

    outs = []
    for h in range(C_HEADS):
        hs = slice(h * C_HD, (h + 1) * C_HD)
        st = st_ref[h]
        o_h = _dot_nt(qe[:, hs], st.astype(BF16))
        st_ref[h] = st * e_last[:, hs] + _dot_tn(v_bf[:, hs], kdec[:, hs])
        cum_h, q_h, k_h, v_h = cum[:, hs], q[:, hs], kk[:, hs], v[:, hs]
        o_sub = [None] * nsb
        for p, bi in enumerate(order):
            rows = slice(bi * SB, (bi + 1) * SB)
            cum_i, q_i, k_i, v_i = cum_h[rows], q_h[rows], k_h[rows], v_h[rows]
            groups = SB // SUBLANES
            w_rows, spans = [], []
            for s in range(SB):
                g_s = s // SUBLANES
                live = range(0, g_s + 1) if reverse else range(g_s, groups)
                lo, hi = live[0] * SUBLANES, (live[-1] + 1) * SUBLANES
                ok = (sub_r[lo:hi] <= s) if reverse else (sub_r[lo:hi] >= s)
                e = jnp.where(ok, jnp.exp(cum_i[lo:hi] - cum_i[s:s + 1]), 0.0)
                w_rows.append(q_i[lo:hi] * e * k_i[s:s + 1])
                spans.append(live)
            red = _dot(jnp.concatenate(w_rows, axis=0).astype(BF16), ones)
            acc_g = [jnp.zeros((SUBLANES, C_HD), F32) for _ in range(groups)]
            off = 0
            for s in range(SB):
                for g in spans[s]:
                    acc_g[g] = acc_g[g] + red[off:off + SUBLANES] * v_i[s:s + 1]
                    off += SUBLANES
            acc = jnp.concatenate(acc_g, axis=0)
            if p > 0:
                prev = order[p - 1]
                b_row = prev * SB if reverse else prev * SB + SB - 1
                b = cum_h[b_row:b_row + 1]
                if reverse:
                    past = slice((bi + 1) * SB, L)
                else:
                    past = slice(0, bi * SB)
                qi = (q_i * jnp.exp(cum_i - b)).astype(BF16)
                kp = (k_h[past] * jnp.exp(b - cum_h[past])).astype(BF16)
                att = _dot_nt(qi, kp)
                acc = acc + _dot(att.astype(BF16), v_bf[past, hs])
            o_sub[bi] = acc
        outs.append(o_h + jnp.concatenate(o_sub, axis=0))
    o_ref[...] = jnp.concatenate(outs, axis=1)


def _scan(p5, lb, n_lat, n_ctx, batch):
    t_all = p5.shape[0]
    L = C_CHUNK
    w = C_HEADS * C_HD
    nl, nc = n_lat // L, n_ctx // L
    steps = nl + nc

    def row_idx(reverse):
        def idx(b, c):
            if reverse:
                ctx = batch * nl + b * nc + (nc - 1 - c)
                lat = b * nl + (nl - 1 - (c - nc))
            else:
                ctx = batch * nl + b * nc + c
                lat = b * nl + (c - nc)
            return jnp.where(c < nc, ctx, lat)
        return idx

    fwd, bwd = row_idx(False), row_idx(True)
    col = lambda ri, cb: pl.BlockSpec((L, w), lambda b, c: (ri(b, c), cb))
    out = jax.ShapeDtypeStruct((t_all, w), F32)
    state = pltpu.VMEM((C_HEADS, C_HD, C_HD), F32)
    return pl.pallas_call(
        _scan_kernel,
        grid=(batch, steps),
        in_specs=[col(fwd, 0), col(fwd, 1), col(fwd, 3), col(bwd, 0), col(bwd, 2), col(bwd, 3),
                  pl.BlockSpec((1, w), lambda b, c: (0, 0))],
        out_specs=[col(fwd, 0), col(bwd, 0)],
        out_shape=[out, out],
        scratch_shapes=[state, state],
        compiler_params=pltpu.CompilerParams(dimension_semantics=("arbitrary", "arbitrary")),
        name="hgrn_scan",
    )(p5, p5, p5, p5, p5, p5, lb)


def _route_kernel(lg_ref, tri_ref, sel_ref, gate_ref, cnt_ref, tb_ref, carry_ref):
    i = pl.program_id(0)

    @pl.when(i == 0)
    def _():
        carry_ref[...] = jnp.zeros(carry_ref.shape, F32)

    lg = lg_ref[...]
    lane = lax.broadcasted_iota(jnp.int32, lg.shape, 1)
    lane_f = lane.astype(F32)
    x = jnp.where(lane < N_EXPERTS, lg, -jnp.inf)
    vals, idxs, hits = [], [], []
    for _ in range(TOP_K):
        mk = jnp.max(x, axis=-1, keepdims=True)
        ik = jnp.min(jnp.where(x == mk, lane_f, float(LANES)), axis=-1, keepdims=True)
        hit = lane_f == ik
        x = jnp.where(hit, -jnp.inf, x)
        vals.append(mk)
        idxs.append(ik.astype(jnp.int32))
        hits.append(hit)
    member = hits[0] | hits[1] | hits[2] | hits[3]
    member_f = jnp.where(member, 1.0, 0.0)
    start = carry_ref[0:1]
    before = _dot(tri_ref[...], member_f.astype(BF16)) + start
    tile_rows = [start]
    for t0 in range(0, lg.shape[0] - COMBINE_TILE, COMBINE_TILE):
        tile_rows.append(tile_rows[-1] + jnp.sum(member_f[t0:t0 + COMBINE_TILE], axis=0, keepdims=True))
    tile_rows += [jnp.zeros_like(start)] * (MOD_ROWS - len(tile_rows))
    tb_ref[...] = jnp.concatenate(tile_rows, axis=0)
    carry_ref[0:1] = start + jnp.sum(member_f, axis=0, keepdims=True)
    exps = [jnp.exp(v - vals[0]) for v in vals]
    denom = exps[0] + exps[1] + exps[2] + exps[3]
    sel = jnp.zeros(lg.shape, jnp.int32)
    gate = jnp.zeros(lg.shape, F32)
    for k in range(TOP_K):
        rank = jnp.sum(jnp.where(hits[k], before, 0.0), axis=-1, keepdims=True).astype(jnp.int32)
        sel = jnp.where(lane == k, idxs[k], sel)
        sel = jnp.where(lane == TOP_K + k, rank, sel)
        gate = jnp.where(lane == k, exps[k] / denom, gate)
    sel_ref[...] = sel
    gate_ref[...] = gate
    cnt_ref[...] = carry_ref[...]


def _route(logits):
    t = logits.shape[0]
    tm = ROW_TILE
    assert tm // COMBINE_TILE <= MOD_ROWS
    tri = (jnp.arange(tm)[:, None] > jnp.arange(tm)[None, :]).astype(BF16)
    blk = pl.BlockSpec((tm, LANES), lambda i: (i, 0))
    return pl.pallas_call(
        _route_kernel,
        grid=(t // tm,),
        in_specs=[blk, pl.BlockSpec((tm, tm), lambda i: (0, 0))],
        out_specs=[blk, blk, pl.BlockSpec((MOD_ROWS, LANES), lambda i: (0, 0)),
                   pl.BlockSpec((None, MOD_ROWS, LANES), lambda i: (i, 0, 0))],
        out_shape=[
            jax.ShapeDtypeStruct((t, LANES), jnp.int32),
            jax.ShapeDtypeStruct((t, LANES), F32),
            jax.ShapeDtypeStruct((MOD_ROWS, LANES), F32),
            jax.ShapeDtypeStruct((t // tm, MOD_ROWS, LANES), F32),
        ],
        scratch_shapes=[pltpu.VMEM((MOD_ROWS, LANES), F32)],
        compiler_params=pltpu.CompilerParams(dimension_semantics=("arbitrary",)),
        name="route",
    )(logits, tri)


def _dispatch_kernel(zs_ref, runs_ref, v_ref, rows_hbm, x_hbm, rows_smem, slabs, zbuf, sems):
    i = pl.program_id(0)
    slot = i % 2
    tm = v_ref.shape[0] // SUBLANES
    pad = zbuf.shape[0]
    rows_cp = pltpu.make_async_copy(rows_hbm.at[i], rows_smem, sems.at[0])
    rows_cp.start()

    @pl.when(i == 0)
    def _():
        zbuf[...] = jnp.zeros(zbuf.shape, F32)
        for e in range(N_EXPERTS):
            z0 = pl.multiple_of(zs_ref[e] * SUBLANES, SUBLANES)
            pltpu.make_async_copy(zbuf, x_hbm.at[pl.ds(z0, pad), :], sems.at[3]).start()
        for e in range(N_EXPERTS):
            pltpu.make_async_copy(zbuf, x_hbm.at[pl.ds(0, pad), :], sems.at[3]).wait()

    rows_cp.wait()
    slab = slabs.at[slot]

    def place(r, carry):
        row = v_ref[pl.ds(pl.multiple_of(r * SUBLANES, SUBLANES), SUBLANES), :]
        for k in range(TOP_K):
            dst = pl.multiple_of(rows_smem[r * TOP_K + k] * SUBLANES, SUBLANES)
            slab[pl.ds(dst, SUBLANES), :] = row
        return carry

    lax.fori_loop(0, tm, place, 0, unroll=4)
    send = functools.partial(_run_copies, runs_ref, hbm=x_hbm, to_hbm=True)
    send(i, slab=slab, sem=sems.at[1 + slot], wait=False)

    @pl.when(i >= 1)
    def _():
        send(i - 1, slab=slabs.at[1 - slot], sem=sems.at[2 - slot], wait=True)

    @pl.when(i == pl.num_programs(0) - 1)
    def _():
        send(i, slab=slab, sem=sems.at[1 + slot], wait=True)


def _dispatch(v, runs, rows, zero_start, n_rows):
    t = v.shape[0] // SUBLANES
    tm = COMBINE_TILE
    grid_spec = pltpu.PrefetchScalarGridSpec(
        num_scalar_prefetch=2,
        grid=(t // tm,),
        in_specs=[pl.BlockSpec((tm * SUBLANES, LANES), lambda i, *_: (i, 0)), pl.BlockSpec(memory_space=pl.ANY)],
        out_specs=pl.BlockSpec(memory_space=pl.ANY),
        scratch_shapes=[pltpu.SMEM((tm * TOP_K,), jnp.int32), pltpu.VMEM((2, TOP_K * tm * SUBLANES, LANES), F32),
                        pltpu.VMEM((MOE_TILE * SUBLANES, LANES), F32), pltpu.SemaphoreType.DMA((4,))],
    )
    return pl.pallas_call(
        _dispatch_kernel,
        grid_spec=grid_spec,
        out_shape=jax.ShapeDtypeStruct(((n_rows + MOE_TILE) * SUBLANES, LANES), F32),
        compiler_params=pltpu.CompilerParams(dimension_semantics=("arbitrary",)),
        name="dispatch",
    )(zero_start, runs, v, rows)


def _moe_kernel(be_ref, nu_ref, first_ref, x_ref, w1_ref, b1g_ref, b1l_ref, w2_ref, b2_ref, perm_ref, o_ref,
                w1g_scr, w1l_scr, w2_scr):
    i = pl.program_id(0)
    live = i < nu_ref[0]

    @pl.when(jnp.logical_and(live, first_ref[i] == 1))
    def _():
        perm = perm_ref[...]
        group = 2 * LANES
        for b in range(w1_ref.shape[1] // group):
            r = _dot(w1_ref[:, b * group:(b + 1) * group].astype(BF16), perm)
            w1g_scr[:, b * LANES:(b + 1) * LANES] = r[:, :LANES].astype(BF16)
            w1l_scr[:, b * LANES:(b + 1) * LANES] = r[:, LANES:].astype(BF16)
        w2_scr[...] = w2_ref[...].astype(BF16)

    @pl.when(live)
    def _():
        x = _unpack_rows(x_ref, MOE_TILE).astype(BF16)
        hg = _dot(x, w1g_scr[...]) + b1g_ref[...]
        hl = _dot(x, w1l_scr[...]) + b1l_ref[...]
        glu = jnp.minimum(hg, SWIGLU_LIMIT)
        lin = jnp.clip(hl, -SWIGLU_LIMIT, SWIGLU_LIMIT)
        y = glu * jax.nn.sigmoid(SWIGLU_ALPHA * glu) * (lin + 1.0)
        _pack_rows(o_ref, _dot(y.astype(BF16), w2_scr[...]) + b2_ref[...])

    @pl.when(jnp.logical_not(live))
    def _():
        o_ref[...] = jnp.zeros(o_ref.shape, F32)


def _deinterleave_perm():
    src = jnp.arange(2 * LANES)
    dst = jnp.where(src % 2 == 0, src // 2, LANES + src // 2)
    return (dst[:, None] == jnp.arange(2 * LANES)[None, :]).astype(BF16)


def _moe_experts(xb, n_rows, block_expert, n_used, first, layer, w1, b1g, b1l, w2, b2):
    d = w1.shape[2]
    tm = MOE_TILE
    dff = w2.shape[2]
    perm = _deinterleave_perm()
    tile = pl.BlockSpec((tm * SUBLANES, LANES), lambda i, be, nu, fi: (i, 0))
    wspec = lambda k, n: pl.BlockSpec((None, k, n), lambda i, be, nu, fi: (be[i], 0, 0))
    lwspec = lambda k, n: pl.BlockSpec((None, None, k, n), lambda i, be, nu, fi: (layer, be[i], 0, 0))
    grid_spec = pltpu.PrefetchScalarGridSpec(
        num_scalar_prefetch=3,
        grid=(n_rows // tm,),
        in_specs=[
            tile,
            lwspec(d, 2 * dff), wspec(1, dff), wspec(1, dff), lwspec(dff, d), wspec(1, d),
            pl.BlockSpec(perm.shape, lambda i, be, nu, fi: (0, 0)),
        ],
        out_specs=tile,
        scratch_shapes=[pltpu.VMEM((d, dff), BF16), pltpu.VMEM((d, dff), BF16), pltpu.VMEM((dff, d), BF16)],
    )
    return pl.pallas_call(
        _moe_kernel,
        grid_spec=grid_spec,
        out_shape=jax.ShapeDtypeStruct((n_rows * SUBLANES, LANES), F32),
        compiler_params=pltpu.CompilerParams(dimension_semantics=("arbitrary",), vmem_limit_bytes=MOE_VMEM_BYTES),
        name="moe_experts",
    )(block_expert, n_used, first, xb, w1, b1g, b1l, w2, b2, perm)


def _moe(v, logits, layer, w1, b1g, b1l, w2, b2):
    t = logits.shape[0]
    tm = MOE_TILE
    sel, gates, cnt, tile_cnt = _route(logits)
    counts = cnt[0, :N_EXPERTS].astype(jnp.int32)
    padded = (counts + tm - 1) // tm * tm
    pad_end = jnp.cumsum(padded)
    pad_start = pad_end - padded
    n_blocks = t * TOP_K // tm + N_EXPERTS
    n_rows = n_blocks * tm
    block_start = jnp.arange(n_blocks, dtype=jnp.int32) * tm
    block_expert = jnp.minimum(jnp.sum(block_start[:, None] >= pad_end[None, :], axis=1), N_EXPERTS - 1)
    block_expert = block_expert.astype(jnp.int32)
    n_used = (pad_end[-1] // tm).astype(jnp.int32).reshape(1)
    first = jnp.concatenate([jnp.ones((1,), jnp.int32), (block_expert[1:] != block_expert[:-1]).astype(jnp.int32)])
    n_tiles = t // COMBINE_TILE
    per_block = ROW_TILE // COMBINE_TILE
    before = tile_cnt[:, :per_block, :N_EXPERTS].astype(jnp.int32).reshape(n_tiles, N_EXPERTS)
    run_len = jnp.concatenate([before[1:], counts[None, :]], axis=0) - before
    slab_off = jnp.cumsum(run_len, axis=1) - run_len
    expert = sel[:, :TOP_K].reshape(n_tiles, COMBINE_TILE * TOP_K)
    rank = sel[:, TOP_K:2 * TOP_K].reshape(n_tiles, COMBINE_TILE * TOP_K)
    hit = expert[:, :, None] == jnp.arange(N_EXPERTS)[None, None, :]
    slab_row = rank + jnp.sum(jnp.where(hit, (slab_off - before)[:, None, :], 0), axis=-1)
    runs = jnp.concatenate([pad_start[None, :] + before, run_len], axis=1).astype(jnp.int32).reshape(-1)
    rows = slab_row.astype(jnp.int32)
    x_sorted = _dispatch(v, runs, rows, (pad_start + counts).astype(jnp.int32), n_rows)
    y_sorted = _moe_experts(x_sorted, n_rows, block_expert, n_used, first, layer, w1, b1g, b1l, w2, b2)
    return gates, runs, rows, y_sorted


def _final_kernel(runs_ref, h_ref, g2_ref, gate_ref, rows_hbm, y_hbm, g_ref, o_ref, *combine_scratch):
    moe = _moe_combine(runs_ref, rows_hbm, y_hbm, gate_ref, *combine_scratch)
    o_ref[...] = _rms(h_ref[...] + g2_ref[...] * moe, g_ref[...])


def _final_norm(h_all, pending, g, t_lat, n_lat):
    d = h_all.shape[1]
    tm = COMBINE_TILE
    g2, gates, runs, rows, y_sorted = pending
    row = pl.BlockSpec((tm, d), lambda i, *_: (i, 0))
    any_spec = pl.BlockSpec(memory_space=pl.ANY)
    grid_spec = pltpu.PrefetchScalarGridSpec(
        num_scalar_prefetch=1,
        grid=(t_lat // tm,),
        in_specs=[row, _mod_spec(n_lat // tm), pl.BlockSpec((tm, LANES), lambda i, *_: (i, 0)), any_spec, any_spec,
                  pl.BlockSpec((1, d), lambda i, *_: (0, 0))],
        out_specs=row,
        scratch_shapes=_combine_scratch(tm, d),
    )
    return pl.pallas_call(
        _final_kernel,
        grid_spec=grid_spec,
        out_shape=jax.ShapeDtypeStruct((t_lat, d), F32),
        compiler_params=pltpu.CompilerParams(dimension_semantics=("arbitrary",)),
        name="final_norm",
    )(runs, h_all, g2, gates, rows, y_sorted, g.reshape(1, d))


def _rope_tables(n_lat, n_ctx_pad):
    quarter = A_HD // 4
    inv_freq = ROPE_THETA ** (-jnp.arange(quarter, dtype=F32) / quarter)
    t = jnp.arange(n_lat)
    row = (t // GRID_W).astype(F32)
    colp = (t % GRID_W).astype(F32)
    ang_r = row[:, None] * inv_freq
    ang_c = colp[:, None] * inv_freq
    cos64 = jnp.concatenate([jnp.cos(ang_r), jnp.cos(ang_r), jnp.cos(ang_c), jnp.cos(ang_c)], axis=1)
    sin64 = jnp.concatenate([-jnp.sin(ang_r), jnp.sin(ang_r), -jnp.sin(ang_c), jnp.sin(ang_c)], axis=1)
    one, zero = jnp.ones_like(cos64), jnp.zeros_like(sin64)

    def pad(a, fill):
        return jnp.concatenate([a, jnp.full((n_ctx_pad, a.shape[1]), fill, F32)], axis=0)

    cos_a = pad(jnp.concatenate([cos64, cos64], axis=1), 1.0)
    sin_a = pad(jnp.concatenate([sin64, sin64], axis=1), 0.0)
    cos_b = pad(jnp.concatenate([cos64, one], axis=1), 1.0)
    sin_b = pad(jnp.concatenate([sin64, zero], axis=1), 0.0)
    return cos_a, sin_a, cos_b, sin_b


def _prep_uq(w_uq):
    r = w_uq.shape[0]
    w = w_uq.reshape(r, B_HEADS, B_NOPE + B_ROPE)
    w = jnp.pad(w, ((0, 0), (0, 0), (0, 2 * LANES - B_NOPE - B_ROPE)))
    return w.reshape(r, B_HEADS * 2 * LANES).astype(BF16)


def _prep_ukv(w_ukv):
    r = w_ukv.shape[0]
    w = w_ukv.reshape(r, B_HEADS, B_NOPE + B_VD)
    wk = w[:, :, :B_NOPE].reshape(r, B_HEADS * B_NOPE)
    wv = w[:, :, B_NOPE:].reshape(r, B_HEADS * B_VD)
    return wk.astype(BF16), wv.astype(BF16)


def kernel(x, c, ctx, c_ctx, norm_mix_g, norm_ffn_g, w_ada, b_ada, w_in_ab, diff_lambda, diff_subln_g, mla_q_norm_g, mla_kv_norm_g, w_uq, w_ukv, w_out_ab, w_in_c, lb_raw, hgrn_norm_g, w_out_c, w_router, b_router, w_exp1, b_exp1, w_exp2, b_exp2, final_g):
    batch, n_lat, d = x.shape
    n_ctx = ctx.shape[1]
    t_lat = batch * n_lat
    h_all = jnp.concatenate([x.reshape(t_lat, d), ctx.reshape(batch * n_ctx, d)], axis=0)

    cvec = jnp.concatenate([c, c_ctx[None, :], jnp.zeros((MOD_ROWS - batch - 1, d), F32)], axis=0)
    mods = _ada(cvec, w_ada, b_ada)

    lb_p = jax.nn.softmax(lb_raw.astype(F32), axis=0)
    lower_bounds = jnp.cumsum(lb_p, axis=0) - lb_p[0]
    tabs = _rope_tables(n_lat, PREP_TILE)

    kv_len = n_lat + n_ctx
    tq = min(FLASH_TQ, n_lat)
    tk = next(t for t in FLASH_TK_CHOICES if kv_len % t == 0)

    pending = None
    for l in range(DEPTH):
        j = l // 2
        m = mods[l]
        sh1, sc1, g1, sh2, sc2, g2 = [m[:, k * d:(k + 1) * d].reshape(MOD_ROWS, 1, d) for k in range(6)]
        wr = jnp.pad(w_router[l], ((0, 0), (0, LANES - N_EXPERTS)))
        wrh, wrl = _split_hi_lo(wr)
        br = jnp.pad(b_router[l].astype(F32), (0, LANES - N_EXPERTS)).reshape(1, LANES)
        row = lambda w: pl.BlockSpec((ROW_TILE, w), lambda i: (i, 0))
        full = lambda a: pl.BlockSpec(a.shape, lambda i: (0,) * a.ndim)
        if l % 2 == 0:
            w_in, tn = jnp.pad(w_in_ab[j], ((0, 0), (0, AB_PROJ_PAD - w_in_ab.shape[2]))).astype(BF16), AB_PROJ_PAD
        else:
            w_in, tn = w_in_c[j].astype(BF16), 1024
        p, h_all = _inproj(h_all, norm_mix_g[l], sc1, sh1, w_in, n_lat, tn, pending)
        tail = (h_all, g1, norm_ffn_g[l], sc2, sh2, wrh, wrl, br, n_lat)
        if l % 2 == 0:
            lam_init = 0.8 - 0.6 * math.exp(-0.3 * l)
            wuk, wuv = _prep_ukv(w_ukv[j])
            qa, qb, ka, va, kb, vb = _abprep(
                p, tabs, mla_q_norm_g[j].reshape(1, -1), mla_kv_norm_g[j].reshape(1, -1), _prep_uq(w_uq[j]), wuk, wuv,
                n_lat, n_ctx, batch)
            extra = (diff_lambda[j].astype(F32), diff_subln_g[j].reshape(1, A_VD))
            lat = dict(batch=batch, tq=tq, tk=tk, q_blk0=0, q_blk_stride=n_lat // tq, nq=n_lat // tq,
                       kv_blk_stride=kv_len // tk, nkv=kv_len // tk)
            oa = _flash("diff", qa, ka, va, extra, heads=A_HEADS, lam_init=lam_init, **lat)
            ob = _flash("mla", qb, kb, vb, (), heads=B_HEADS, **lat)
            if l != DEPTH - 1:
                cq = dict(batch=batch, tq=n_ctx, tk=n_ctx, q_blk0=t_lat // n_ctx, q_blk_stride=1, nq=1,
                          kv_blk_stride=kv_len // n_ctx, nkv=1)
                oa = _flash("diff", qa, ka, va, extra, heads=A_HEADS, lam_init=lam_init, out_init=oa, **cq)
                ob = _flash("mla", qb, kb, vb, (), heads=B_HEADS, out_init=ob, **cq)
            wo = w_out_ab[j].astype(BF16)
            wa, wb = wo[:A_HEADS * A_VD], wo[A_HEADS * A_VD:]
            h_mid, v_ffn, logits = _outproj(
                "ab", (oa, ob, wa, wb), [row(oa.shape[1]), row(ob.shape[1]), full(wa), full(wb)], *tail)
        else:
            p5 = p
            lb = lower_bounds[l].reshape(1, -1)
            o_f, o_b = _scan(p5, lb, n_lat, n_ctx, batch)
            ng = hgrn_norm_g[j].reshape(1, C_HD)
            wo = w_out_c[j].astype(BF16)
            gate_spec = pl.BlockSpec((ROW_TILE, d), lambda i: (i, 4))
            h_mid, v_ffn, logits = _outproj(
                "c", (o_f, o_b, p5, ng, wo), [row(d), row(d), gate_spec, full(ng), full(wo)], *tail)

        b1 = b_exp1[l].astype(F32)
        b1g, b1l = b1[:, None, 0::2], b1[:, None, 1::2]
        pending = (g2,) + _moe(v_ffn, logits, l, w_exp1, b1g, b1l, w_exp2, b_exp2[l].astype(F32)[:, None, :])
        h_all = h_mid

    return _final_norm(h_all, pending, final_g, t_lat, n_lat).reshape(batch, n_lat, d)
```

```python
import functools
import math

import jax
import jax.numpy as jnp
from jax import lax
from jax.experimental import pallas as pl
from jax.experimental.pallas import tpu as pltpu

F32 = jnp.float32
BF16 = jnp.bfloat16

D_MODEL = 1024
DEPTH = 4
GRID_W = 64
ROPE_THETA = 10000.0
NORM_EPS = 1e-6

A_HEADS = 4
A_HD = 64
A_VD = 128
A_SCALE = A_HD ** -0.5
B_HEADS = 4
B_NOPE = 128
B_ROPE = 64
B_VD = 128
B_Q_RANK = 256
B_KV_RANK = 128
B_SCALE = (B_NOPE + B_ROPE) ** -0.5
LOG2E = math.log2(math.e)
AB_PROJ_PAD = 2048

C_HEADS = 8
C_HD = 128
C_CHUNK = 64
C_SUB = 16

N_EXPERTS = 32
TOP_K = 4
SWIGLU_ALPHA = 1.702
SWIGLU_LIMIT = 7.0

LANES = 128
SUBLANES = 8
ROW_TILE = 512
PREP_TILE = 256
FLASH_SUB = 256
FLASH_TQ = 2048
FLASH_TK_CHOICES = (2816, 768, 512, 256)
MOE_TILE = 512
COMBINE_TILE = 512
MOD_ROWS = 8
MOE_VMEM_BYTES = 60 * 1024 * 1024


def _split_hi_lo(x):
    hi = x.astype(BF16)
    lo = (x - hi.astype(F32)).astype(BF16)
    return hi, lo


def _dot(a, b):
    return jnp.dot(a, b, preferred_element_type=F32)


def _dot_nt(a, b):
    return lax.dot_general(a, b, (((1,), (1,)), ((), ())), preferred_element_type=F32)


def _dot_tn(a, b):
    return lax.dot_general(a, b, (((0,), (0,)), ((), ())), preferred_element_type=F32)


def _rms(x, g):
    ms = jnp.mean(x * x, axis=-1, keepdims=True)
    return x * lax.rsqrt(ms + NORM_EPS) * g


def _ada_kernel(c_ref, w_ref, b_ref, o_ref):
    c = c_ref[...]
    s = c * jax.nn.sigmoid(c)
    s_hi, s_lo = _split_hi_lo(s)
    w_hi, w_lo = _split_hi_lo(w_ref[...])
    o_ref[...] = _dot(s_hi, w_hi) + _dot(s_lo, w_hi) + _dot(s_hi, w_lo) + b_ref[...]


def _ada(cvec, w_ada, b_ada):
    depth, d, n6 = w_ada.shape
    tn = 1536
    return pl.pallas_call(
        _ada_kernel,
        grid=(depth, n6 // tn),
        in_specs=[
            pl.BlockSpec((MOD_ROWS, d), lambda l, j: (0, 0)),
            pl.BlockSpec((None, d, tn), lambda l, j: (l, 0, j)),
            pl.BlockSpec((None, 1, tn), lambda l, j: (l, 0, j)),
        ],
        out_specs=pl.BlockSpec((None, MOD_ROWS, tn), lambda l, j: (l, 0, j)),
        out_shape=jax.ShapeDtypeStruct((depth, MOD_ROWS, n6), F32),
        name="ada_mod",
    )(cvec, w_ada, b_ada.reshape(depth, 1, n6))


def _unpack_rows(ref, tm):
    return jnp.concatenate([ref[pl.ds(s, tm, stride=SUBLANES), :] for s in range(SUBLANES)], axis=1)


def _pack_rows(ref, x):
    tm = x.shape[0]
    for s in range(SUBLANES):
        ref[pl.ds(s, tm, stride=SUBLANES), :] = x[:, s * LANES:(s + 1) * LANES]


RUN_STRIDE = 2 * N_EXPERTS
RUN_BITS = tuple(1 << b for b in range(COMBINE_TILE.bit_length() - 1, -1, -1))


def _run_copies(runs_ref, tile, hbm, slab, sem, *, to_hbm, wait):
    base = tile * RUN_STRIDE

    def per_expert(e, cur):
        row0 = runs_ref[base + e]
        n = runs_ref[base + N_EXPERTS + e]
        pos = jnp.int32(0)
        for bit in RUN_BITS:
            take = (n & bit) != 0

            @pl.when(take)
            def _():
                h = hbm.at[pl.ds(pl.multiple_of((row0 + pos) * SUBLANES, SUBLANES), bit * SUBLANES), :]
                s = slab.at[pl.ds(pl.multiple_of((cur + pos) * SUBLANES, SUBLANES), bit * SUBLANES), :]
                cp = pltpu.make_async_copy(s, h, sem) if to_hbm else pltpu.make_async_copy(h, s, sem)
                if wait:
                    cp.wait()
                else:
                    cp.start()

            pos = pos + jnp.where(take, bit, 0)
        return cur + n

    lax.fori_loop(0, N_EXPERTS, per_expert, jnp.int32(0))


def _moe_combine(runs_ref, rows_hbm, y_hbm, gate_ref, rows_smem, slabs, ybuf, sems):
    i = pl.program_id(0)
    slot = i % 2
    tm = ybuf.shape[1] // SUBLANES
    rows_cp = pltpu.make_async_copy(rows_hbm.at[i], rows_smem, sems.at[0])
    rows_cp.start()
    fetch = functools.partial(_run_copies, runs_ref, hbm=y_hbm, to_hbm=False)

    @pl.when(i == 0)
    def _():
        fetch(0, slab=slabs.at[0], sem=sems.at[1], wait=False)

    @pl.when(i + 1 < pl.num_programs(0))
    def _():
        fetch(i + 1, slab=slabs.at[1 - slot], sem=sems.at[2 - slot], wait=False)

    fetch(i, slab=slabs.at[slot], sem=sems.at[1 + slot], wait=True)
    rows_cp.wait()
    slab = slabs.at[slot]

    def place(r, carry):
        for k in range(TOP_K):
            src = pl.multiple_of(rows_smem[r * TOP_K + k] * SUBLANES, SUBLANES)
            ybuf[k, pl.ds(pl.multiple_of(r * SUBLANES, SUBLANES), SUBLANES), :] = slab[pl.ds(src, SUBLANES), :]
        return carry

    lax.fori_loop(0, tm, place, 0, unroll=4)
    gates = gate_ref[...]
    acc = gates[:, 0:1] * _unpack_rows(ybuf.at[0], tm)
    for k in range(1, TOP_K):
        acc = acc + gates[:, k:k + 1] * _unpack_rows(ybuf.at[k], tm)
    return acc


def _inproj_kernel(h_ref, g_ref, sc_ref, sh_ref, w_ref, o_ref, u_scr):
    @pl.when(pl.program_id(1) == 0)
    def _():
        u = _rms(h_ref[...], g_ref[...]) * (1.0 + sc_ref[...]) + sh_ref[...]
        u_scr[...] = u.astype(BF16)

    o_ref[...] = _dot(u_scr[...], w_ref[...])


def _inproj_moe_kernel(runs_ref, h_ref, g2_ref, gate_ref, rows_hbm, y_hbm, g_ref, sc_ref, sh_ref, w_ref, o_ref,
                       hn_ref, u_scr, *combine_scratch):
    @pl.when(pl.program_id(1) == 0)
    def _():
        moe = _moe_combine(runs_ref, rows_hbm, y_hbm, gate_ref, *combine_scratch)
        h = h_ref[...] + g2_ref[...] * moe
        hn_ref[...] = h
        u = _rms(h, g_ref[...]) * (1.0 + sc_ref[...]) + sh_ref[...]
        u_scr[...] = u.astype(BF16)

    o_ref[...] = _dot(u_scr[...], w_ref[...])


def _mod_spec(n_lat_blocks):
    def idx(i, *_):
        return (jnp.minimum(i // n_lat_blocks, 2), 0, 0)

    return pl.BlockSpec((None, 1, D_MODEL), idx)


def _combine_scratch(tm, d):
    assert d == SUBLANES * LANES and tm == COMBINE_TILE
    return [pltpu.SMEM((tm * TOP_K,), jnp.int32), pltpu.VMEM((2, TOP_K * tm * SUBLANES, LANES), F32),
            pltpu.VMEM((TOP_K, tm * SUBLANES, LANES), F32), pltpu.SemaphoreType.DMA((3,))]


def _inproj(h_all, gain, sc, sh, w_bf, n_lat, tn, pending=None):
    t_all, d = h_all.shape
    nout = w_bf.shape[1]
    tm = COMBINE_TILE
    mod = _mod_spec(n_lat // tm)
    row = pl.BlockSpec((tm, d), lambda i, j, *_: (i, 0))
    tail_specs = [pl.BlockSpec((1, d), lambda i, j, *_: (0, 0)), mod, mod,
                  pl.BlockSpec((d, tn), lambda i, j, *_: (0, j))]
    tail = (gain.reshape(1, d), sc, sh, w_bf)
    p_spec = pl.BlockSpec((tm, tn), lambda i, j, *_: (i, j))
    p_shape = jax.ShapeDtypeStruct((t_all, nout), F32)
    params = pltpu.CompilerParams(dimension_semantics=("arbitrary", "arbitrary"))
    if pending is None:
        p = pl.pallas_call(
            _inproj_kernel,
            grid=(t_all // tm, nout // tn),
            in_specs=[row] + tail_specs,
            out_specs=p_spec,
            out_shape=p_shape,
            scratch_shapes=[pltpu.VMEM((tm, d), BF16)],
            compiler_params=params,
            name="inproj",
        )(h_all, *tail)
        return p, h_all
    g2, gates, runs, rows, y_sorted = pending
    any_spec = pl.BlockSpec(memory_space=pl.ANY)
    grid_spec = pltpu.PrefetchScalarGridSpec(
        num_scalar_prefetch=1,
        grid=(t_all // tm, nout // tn),
        in_specs=[row, mod, pl.BlockSpec((tm, LANES), lambda i, j, *_: (i, 0)), any_spec, any_spec] + tail_specs,
        out_specs=[p_spec, row],
        scratch_shapes=[pltpu.VMEM((tm, d), BF16)] + _combine_scratch(tm, d),
    )
    return pl.pallas_call(
        _inproj_moe_kernel,
        grid_spec=grid_spec,
        out_shape=[p_shape, jax.ShapeDtypeStruct((t_all, d), F32)],
        compiler_params=params,
        name="inproj_moe",
    )(runs, h_all, g2, gates, rows, y_sorted, *tail)


def _rope(x, cos, sin):
    n = x.shape[-1]
    lane = lax.broadcasted_iota(jnp.int32, x.shape, 1)
    first = (lane // 16) % 2 == 0
    partner = jnp.where(first, pltpu.roll(x, n - 16, 1), pltpu.roll(x, 16, 1))
    return x * cos + partner * sin


def _abprep_kernel(p_ref, cosa_ref, sina_ref, cosb_ref, sinb_ref, qg_ref, kvg_ref, wuq_ref, wuk_ref, wuv_ref,
                   qa_ref, qb_ref, ka_ref, va_ref, kb_ref, vb_ref):
    cosa, sina = cosa_ref[...], sina_ref[...]
    cosb, sinb = cosb_ref[...], sinb_ref[...]
    n_a = 2 * A_HEADS * A_HD
    qa = [_rope(p_ref[:, c:c + LANES], cosa, sina) * (A_SCALE * LOG2E) for c in range(0, n_a, LANES)]
    qa_ref[...] = jnp.concatenate(qa, axis=1).astype(BF16)
    ka = [_rope(p_ref[:, n_a + c:n_a + c + LANES], cosa, sina) for c in range(0, n_a, LANES)]
    ka_ref[...] = jnp.concatenate(ka, axis=1).astype(BF16)
    va_ref[...] = p_ref[:, 2 * n_a:2 * n_a + A_HEADS * A_VD].astype(BF16)

    off = 2 * n_a + A_HEADS * A_VD
    cq = _rms(p_ref[:, off:off + B_Q_RANK], qg_ref[...]).astype(BF16)
    qf = _dot(cq, wuq_ref[...])
    ckv = _rms(p_ref[:, off + B_Q_RANK:off + B_Q_RANK + B_KV_RANK], kvg_ref[...]).astype(BF16)
    kn = _dot(ckv, wuk_ref[...])
    vb_ref[...] = _dot(ckv, wuv_ref[...]).astype(BF16)
    kr_off = off + B_Q_RANK + B_KV_RANK
    krr = _rope(p_ref[:, kr_off:kr_off + LANES], cosb, sinb)
    qb, kb = [], []
    for h in range(B_HEADS):
        qb.append(qf[:, 2 * LANES * h:2 * LANES * h + LANES] * (B_SCALE * LOG2E))
        qb.append(_rope(qf[:, 2 * LANES * h + LANES:2 * LANES * (h + 1)], cosb, sinb) * (B_SCALE * LOG2E))
        kb.append(kn[:, LANES * h:LANES * (h + 1)])
        kb.append(krr)
    qb_ref[...] = jnp.concatenate(qb, axis=1).astype(BF16)
    kb_ref[...] = jnp.concatenate(kb, axis=1).astype(BF16)


def _abprep(p, tabs, qg, kvg, wuq, wuk, wuv, n_lat, n_ctx, batch):
    t_all = p.shape[0]
    tm = PREP_TILE
    nlb = n_lat // tm
    ncb = n_ctx // tm
    kvb = nlb + ncb

    def tab_idx(i):
        return (jnp.where(i < batch * nlb, i % nlb, nlb), 0)

    def kv_idx(i):
        lat = (i // nlb) * kvb + ncb + i % nlb
        j = i - batch * nlb
        ctx = (j // ncb) * kvb + j % ncb
        return (jnp.where(i < batch * nlb, lat, ctx), 0)

    tab_spec = pl.BlockSpec((tm, LANES), tab_idx)
    full = lambda a: pl.BlockSpec(a.shape, lambda i: (0,) * a.ndim)
    wq, wk, wv = 2 * A_HEADS * A_HD, B_HEADS * 2 * LANES, A_HEADS * A_VD
    kv_rows = batch * (n_lat + n_ctx)
    return pl.pallas_call(
        _abprep_kernel,
        grid=(t_all // tm,),
        in_specs=[pl.BlockSpec((tm, AB_PROJ_PAD), lambda i: (i, 0)), tab_spec, tab_spec, tab_spec, tab_spec,
                  full(qg), full(kvg), full(wuq), full(wuk), full(wuv)],
        out_specs=[
            pl.BlockSpec((tm, wq), lambda i: (i, 0)),
            pl.BlockSpec((tm, wk), lambda i: (i, 0)),
            pl.BlockSpec((tm, wq), kv_idx),
            pl.BlockSpec((tm, wv), kv_idx),
            pl.BlockSpec((tm, wk), kv_idx),
            pl.BlockSpec((tm, wv), kv_idx),
        ],
        out_shape=[
            jax.ShapeDtypeStruct((t_all, wq), BF16),
            jax.ShapeDtypeStruct((t_all, wk), BF16),
            jax.ShapeDtypeStruct((kv_rows, wq), BF16),
            jax.ShapeDtypeStruct((kv_rows, wv), BF16),
            jax.ShapeDtypeStruct((kv_rows, wk), BF16),
            jax.ShapeDtypeStruct((kv_rows, wv), BF16),
        ],
        name="ab_prep",
    )(p, *tabs, qg, kvg, wuq, wuk, wuv)


def _flash_steps(j, q_maps, k_ref, v_ref, m_refs, acc_refs):
    @pl.when(j == 0)
    def _():
        for m_ref, acc_ref in zip(m_refs, acc_refs):
            m_ref[...] = jnp.full(m_ref.shape, -jnp.inf, F32)
            acc_ref[...] = jnp.zeros(acc_ref.shape, F32)

    k = k_ref[...]
    v = v_ref[...]
    v_ext = jnp.concatenate([v, jnp.ones(v.shape, BF16)], axis=1)
    tq = m_refs[0].shape[0]
    sub = min(FLASH_SUB, tq)
    for r0 in range(0, tq, sub):
        rows = slice(r0, r0 + sub)
        for q_of, m_ref, acc_ref in zip(q_maps, m_refs, acc_refs):
            s = _dot_nt(q_of(rows), k)
            m_prev = m_ref[rows, :]
            m_new = jnp.maximum(m_prev, jnp.max(s, axis=-1, keepdims=True))
            p = jnp.exp2(s - m_new).astype(BF16)
            acc_ref[rows, :] = jnp.exp2(m_prev - m_new) * acc_ref[rows, :] + _dot(p, v_ext)
            m_ref[rows, :] = m_new


def _flash_result(acc_ref):
    wv = acc_ref.shape[1] // 2
    return acc_ref[:, :wv] / acc_ref[:, wv:]


def _flash_diff_kernel(q_ref, k_ref, v_ref, lam_ref, g_ref, o_ref, m1_ref, m2_ref, acc1_ref, acc2_ref, *, lam_init):
    j = pl.program_id(3)

    def q_half(upper):
        def get(rows):
            q = q_ref[rows, :]
            lane = lax.broadcasted_iota(jnp.int32, q.shape, 1)
            return jnp.where((lane >= A_HD) == upper, q, jnp.zeros_like(q))
        return get

    _flash_steps(j, (q_half(False), q_half(True)), k_ref, v_ref, (m1_ref, m2_ref), (acc1_ref, acc2_ref))

    @pl.when(j == pl.num_programs(3) - 1)
    def _():
        lf = lam_ref[...]
        lam = (jnp.exp(jnp.sum(lf[0:1] * lf[1:2], axis=-1, keepdims=True))
               - jnp.exp(jnp.sum(lf[2:3] * lf[3:4], axis=-1, keepdims=True)) + lam_init)
        o = _flash_result(acc1_ref) - lam * _flash_result(acc2_ref)
        o_ref[...] = (_rms(o, g_ref[...]) * (1.0 - lam_init)).astype(o_ref.dtype)


def _flash_mla_kernel(q_ref, k_ref, v_ref, o_ref, m_ref, acc_ref):
    j = pl.program_id(3)
    _flash_steps(j, (lambda rows: q_ref[rows, :],), k_ref, v_ref, (m_ref,), (acc_ref,))

    @pl.when(j == pl.num_programs(3) - 1)
    def _():
        o_ref[...] = _flash_result(acc_ref).astype(o_ref.dtype)


def _flash(kind, q, k, v, extra, *, batch, heads, tq, tk, q_blk0, q_blk_stride, nq, kv_blk_stride, nkv, lam_init=0.0,
           out_init=None):
    wq = q.shape[1] // heads
    wv = v.shape[1] // heads
    n_maps = 2 if kind == "diff" else 1
    q_spec = pl.BlockSpec((tq, wq), lambda b, h, i, j: (q_blk0 + b * q_blk_stride + i, h))
    k_spec = pl.BlockSpec((tk, wq), lambda b, h, i, j: (b * kv_blk_stride + j, h))
    v_spec = pl.BlockSpec((tk, wv), lambda b, h, i, j: (b * kv_blk_stride + j, h))
    o_spec = pl.BlockSpec((tq, wv), lambda b, h, i, j: (q_blk0 + b * q_blk_stride + i, h))
    in_specs = [q_spec, k_spec, v_spec]
    if kind == "diff":
        body = functools.partial(_flash_diff_kernel, lam_init=lam_init)
        in_specs += [pl.BlockSpec(e.shape, lambda b, h, i, j: (0, 0)) for e in extra]
    else:
        body = _flash_mla_kernel
    args = [q, k, v, *extra]
    aliases = {}
    if out_init is not None:
        n_in = len(args)
        in_specs.append(pl.BlockSpec(memory_space=pl.ANY))
        args.append(out_init)
        aliases = {n_in: 0}
        inner = body
        body = lambda *refs: inner(*refs[:n_in], *refs[n_in + 1:])
    return pl.pallas_call(
        body,
        grid=(batch, heads, nq, nkv),
        in_specs=in_specs,
        out_specs=o_spec,
        input_output_aliases=aliases,
        out_shape=jax.ShapeDtypeStruct((q.shape[0], heads * wv), BF16),
        scratch_shapes=([pltpu.VMEM((tq, 1), F32) for _ in range(n_maps)]
                        + [pltpu.VMEM((tq, 2 * wv), F32) for _ in range(n_maps)]),
        compiler_params=pltpu.CompilerParams(
            dimension_semantics=("arbitrary", "arbitrary", "arbitrary", "arbitrary")),
        name="flash_" + kind,
    )(*args)


def _out_epilogue(y, h_ref, g1_ref, nf_ref, sc2_ref, sh2_ref, wrh_ref, wrl_ref, br_ref, hn_ref, v_ref, lg_ref):
    hn = h_ref[...] + g1_ref[...] * y
    hn_ref[...] = hn
    v = _rms(hn, nf_ref[...]) * (1.0 + sc2_ref[...]) + sh2_ref[...]
    _pack_rows(v_ref, v)
    v_hi, v_lo = _split_hi_lo(v)
    wrh = wrh_ref[...]
    lg_ref[...] = _dot(v_hi, wrh) + _dot(v_lo, wrh) + _dot(v_hi, wrl_ref[...]) + br_ref[...]


def _about_kernel(oa_ref, ob_ref, wa_ref, wb_ref, *rest):
    y = _dot(oa_ref[...], wa_ref[...]) + _dot(ob_ref[...], wb_ref[...])
    _out_epilogue(y, *rest)


def _cout_kernel(of_ref, obk_ref, gate_ref, ng_ref, w_ref, *rest):
    o = of_ref[...] + obk_ref[...]
    ng = ng_ref[...]
    parts = [_rms(o[:, c:c + C_HD], ng) for c in range(0, C_HEADS * C_HD, C_HD)]
    g = gate_ref[...]
    x = jnp.concatenate(parts, axis=1) * (g * jax.nn.sigmoid(g))
    _out_epilogue(_dot(x.astype(BF16), w_ref[...]), *rest)


def _outproj(kind, ins, in_specs, h_all, g1, nf, sc2, sh2, wrh, wrl, br, n_lat):
    t_all, d = h_all.shape
    tm = ROW_TILE
    mod = _mod_spec(n_lat // tm)
    row = lambda w: pl.BlockSpec((tm, w), lambda i: (i, 0))
    full = lambda a: pl.BlockSpec(a.shape, lambda i: (0,) * a.ndim)
    nf = nf.reshape(1, d)
    return pl.pallas_call(
        _about_kernel if kind == "ab" else _cout_kernel,
        grid=(t_all // tm,),
        in_specs=in_specs + [row(d), mod, full(nf), mod, mod, full(wrh), full(wrl), full(br)],
        out_specs=[row(d), pl.BlockSpec((tm * SUBLANES, LANES), lambda i: (i, 0)), row(LANES)],
        out_shape=[
            jax.ShapeDtypeStruct((t_all, d), F32),
            jax.ShapeDtypeStruct((t_all * SUBLANES, LANES), F32),
            jax.ShapeDtypeStruct((t_all, LANES), F32),
        ],
        name="outproj_" + kind,
    )(*ins, h_all, g1, nf, sc2, sh2, wrh, wrl, br)


def _cumsum_rows(tri_bf, x):
    hi = x.astype(BF16)
    r1 = x - hi.astype(F32)
    mid = r1.astype(BF16)
    lo = (r1 - mid.astype(F32)).astype(BF16)
    return _dot(tri_bf, hi) + _dot(tri_bf, mid) + _dot(tri_bf, lo)


def _scan_kernel(qf_ref, zf_ref, vf_ref, qb_ref, zb_ref, vb_ref, lb_ref, of_ref, ob_ref, stf_ref, stb_ref):
    _scan_chunk(qf_ref, zf_ref, vf_ref, lb_ref, of_ref, stf_ref, reverse=False)
    _scan_chunk(qb_ref, zb_ref, vb_ref, lb_ref, ob_ref, stb_ref, reverse=True)


def _scan_chunk(q_ref, z_ref, v_ref, lb_ref, o_ref, st_ref, *, reverse):
    c = pl.program_id(1)

    @pl.when(c == 0)
    def _():
        st_ref[...] = jnp.zeros(st_ref.shape, F32)

    L, SB = C_CHUNK, C_SUB
    nsb = L // SB
    lb = lb_ref[...]
    f = lb + (1.0 - lb) * jax.nn.sigmoid(z_ref[...])
    kk = 1.0 - f
    lf = jnp.log(f)
    r_i = lax.broadcasted_iota(jnp.int32, (L, L), 0)
    c_i = lax.broadcasted_iota(jnp.int32, (L, L), 1)
    tri = (c_i >= r_i) if reverse else (c_i <= r_i)
    cum = _cumsum_rows(tri.astype(BF16), lf)
    last_row = 0 if reverse else L - 1
    last = cum[last_row:last_row + 1]
    q = q_ref[...]
    v = v_ref[...]
    qe = (q * jnp.exp(cum)).astype(BF16)
    kdec = (kk * jnp.exp(last - cum)).astype(BF16)
    e_last = jnp.exp(last)
    v_bf = v.astype(BF16)
    ones = jnp.ones((C_HD, C_HD), BF16)
    sub_r = lax.broadcasted_iota(jnp.int32, (SB, C_HD), 0)
    order = list(range(nsb - 1, -1, -1)) if reverse else list(range(nsb))

    outs = []
    for h in range(C_HEADS):
        hs = slice(h * C_HD, (h + 1) * C_HD)
        st = st_ref[h]
        o_h = _dot_nt(qe[:, hs], st.astype(BF16))
        st_ref[h] = st * e_last[:, hs] + _dot_tn(v_bf[:, hs], kdec[:, hs])
        cum_h, q_h, k_h, v_h = cum[:, hs], q[:, hs], kk[:, hs], v[:, hs]
        o_sub = [None] * nsb
        for p, bi in enumerate(order):
            rows = slice(bi * SB, (bi + 1) * SB)
            cum_i, q_i, k_i, v_i = cum_h[rows], q_h[rows], k_h[rows], v_h[rows]
            groups = SB // SUBLANES
            w_rows, spans = [], []
            for s in range(SB):
                g_s = s // SUBLANES
                live = range(0, g_s + 1) if reverse else range(g_s, groups)
                lo, hi = live[0] * SUBLANES, (live[-1] + 1) * SUBLANES
                ok = (sub_r[lo:hi] <= s) if reverse else (sub_r[lo:hi] >= s)
                e = jnp.where(ok, jnp.exp(cum_i[lo:hi] - cum_i[s:s + 1]), 0.0)
                w_rows.append(q_i[lo:hi] * e * k_i[s:s + 1])
                spans.append(live)
            red = _dot(jnp.concatenate(w_rows, axis=0).astype(BF16), ones)
            acc_g = [jnp.zeros((SUBLANES, C_HD), F32) for _ in range(groups)]
            off = 0
            for s in range(SB):
                for g in spans[s]:
                    acc_g[g] = acc_g[g] + red[off:off + SUBLANES] * v_i[s:s + 1]
                    off += SUBLANES
            acc = jnp.concatenate(acc_g, axis=0)
            if p > 0:
                prev = order[p - 1]
                b_row = prev * SB if reverse else prev * SB + SB - 1
                b = cum_h[b_row:b_row + 1]
                if reverse:
                    past = slice((bi + 1) * SB, L)
                else:
                    past = slice(0, bi * SB)
                qi = (q_i * jnp.exp(cum_i - b)).astype(BF16)
                kp = (k_h[past] * jnp.exp(b - cum_h[past])).astype(BF16)
                att = _dot_nt(qi, kp)
                acc = acc + _dot(att.astype(BF16), v_bf[past, hs])
            o_sub[bi] = acc
        outs.append(o_h + jnp.concatenate(o_sub, axis=0))
    o_ref[...] = jnp.concatenate(outs, axis=1)


def _scan(p5, lb, n_lat, n_ctx, batch):
    t_all = p5.shape[0]
    L = C_CHUNK
    w = C_HEADS * C_HD
    nl, nc = n_lat // L, n_ctx // L
    steps = nl + nc

    def row_idx(reverse):
        def idx(b, c):
            if reverse:
                ctx = batch * nl + b * nc + (nc - 1 - c)
                lat = b * nl + (nl - 1 - (c - nc))
            else:
                ctx = batch * nl + b * nc + c
                lat = b * nl + (c - nc)
            return jnp.where(c < nc, ctx, lat)
        return idx

    fwd, bwd = row_idx(False), row_idx(True)
    col = lambda ri, cb: pl.BlockSpec((L, w), lambda b, c: (ri(b, c), cb))
    out = jax.ShapeDtypeStruct((t_all, w), F32)
    state = pltpu.VMEM((C_HEADS, C_HD, C_HD), F32)
    return pl.pallas_call(
        _scan_kernel,
        grid=(batch, steps),
        in_specs=[col(fwd, 0), col(fwd, 1), col(fwd, 3), col(bwd, 0), col(bwd, 2), col(bwd, 3),
                  pl.BlockSpec((1, w), lambda b, c: (0, 0))],
        out_specs=[col(fwd, 0), col(bwd, 0)],
        out_shape=[out, out],
        scratch_shapes=[state, state],
        compiler_params=pltpu.CompilerParams(dimension_semantics=("arbitrary", "arbitrary")),
        name="hgrn_scan",
    )(p5, p5, p5, p5, p5, p5, lb)


def _route_kernel(lg_ref, tri_ref, sel_ref, gate_ref, cnt_ref, tb_ref, carry_ref):
    i = pl.program_id(0)

    @pl.when(i == 0)
    def _():
        carry_ref[...] = jnp.zeros(carry_ref.shape, F32)

    lg = lg_ref[...]
    lane = lax.broadcasted_iota(jnp.int32, lg.shape, 1)
    lane_f = lane.astype(F32)
    x = jnp.where(lane < N_EXPERTS, lg, -jnp.inf)
    vals, idxs, hits = [], [], []
    for _ in range(TOP_K):
        mk = jnp.max(x, axis=-1, keepdims=True)
        ik = jnp.min(jnp.where(x == mk, lane_f, float(LANES)), axis=-1, keepdims=True)
        hit = lane_f == ik
        x = jnp.where(hit, -jnp.inf, x)
        vals.append(mk)
        idxs.append(ik.astype(jnp.int32))
        hits.append(hit)
    member = hits[0] | hits[1] | hits[2] | hits[3]
    member_f = jnp.where(member, 1.0, 0.0)
    start = carry_ref[0:1]
    before = _dot(tri_ref[...], member_f.astype(BF16)) + start
    tile_rows = [start]
    for t0 in range(0, lg.shape[0] - COMBINE_TILE, COMBINE_TILE):
        tile_rows.append(tile_rows[-1] + jnp.sum(member_f[t0:t0 + COMBINE_TILE], axis=0, keepdims=True))
    tile_rows += [jnp.zeros_like(start)] * (MOD_ROWS - len(tile_rows))
    tb_ref[...] = jnp.concatenate(tile_rows, axis=0)
    carry_ref[0:1] = start + jnp.sum(member_f, axis=0, keepdims=True)
    exps = [jnp.exp(v - vals[0]) for v in vals]
    denom = exps[0] + exps[1] + exps[2] + exps[3]
    sel = jnp.zeros(lg.shape, jnp.int32)
    gate = jnp.zeros(lg.shape, F32)
    for k in range(TOP_K):
        rank = jnp.sum(jnp.where(hits[k], before, 0.0), axis=-1, keepdims=True).astype(jnp.int32)
        sel = jnp.where(lane == k, idxs[k], sel)
        sel = jnp.where(lane == TOP_K + k, rank, sel)
        gate = jnp.where(lane == k, exps[k] / denom, gate)
    sel_ref[...] = sel
    gate_ref[...] = gate
    cnt_ref[...] = carry_ref[...]


def _route(logits):
    t = logits.shape[0]
    tm = ROW_TILE
    assert tm // COMBINE_TILE <= MOD_ROWS
    tri = (jnp.arange(tm)[:, None] > jnp.arange(tm)[None, :]).astype(BF16)
    blk = pl.BlockSpec((tm, LANES), lambda i: (i, 0))
    return pl.pallas_call(
        _route_kernel,
        grid=(t // tm,),
        in_specs=[blk, pl.BlockSpec((tm, tm), lambda i: (0, 0))],
        out_specs=[blk, blk, pl.BlockSpec((MOD_ROWS, LANES), lambda i: (0, 0)),
                   pl.BlockSpec((None, MOD_ROWS, LANES), lambda i: (i, 0, 0))],
        out_shape=[
            jax.ShapeDtypeStruct((t, LANES), jnp.int32),
            jax.ShapeDtypeStruct((t, LANES), F32),
            jax.ShapeDtypeStruct((MOD_ROWS, LANES), F32),
            jax.ShapeDtypeStruct((t // tm, MOD_ROWS, LANES), F32),
        ],
        scratch_shapes=[pltpu.VMEM((MOD_ROWS, LANES), F32)],
        compiler_params=pltpu.CompilerParams(dimension_semantics=("arbitrary",)),
        name="route",
    )(logits, tri)


def _dispatch_kernel(zs_ref, runs_ref, v_ref, rows_hbm, x_hbm, rows_smem, slabs, zbuf, sems):
    i = pl.program_id(0)
    slot = i % 2
    tm = v_ref.shape[0] // SUBLANES
    pad = zbuf.shape[0]
    rows_cp = pltpu.make_async_copy(rows_hbm.at[i], rows_smem, sems.at[0])
    rows_cp.start()

    @pl.when(i == 0)
    def _():
        zbuf[...] = jnp.zeros(zbuf.shape, F32)
        for e in range(N_EXPERTS):
            z0 = pl.multiple_of(zs_ref[e] * SUBLANES, SUBLANES)
            pltpu.make_async_copy(zbuf, x_hbm.at[pl.ds(z0, pad), :], sems.at[3]).start()
        for e in range(N_EXPERTS):
            pltpu.make_async_copy(zbuf, x_hbm.at[pl.ds(0, pad), :], sems.at[3]).wait()

    rows_cp.wait()
    slab = slabs.at[slot]

    def place(r, carry):
        row = v_ref[pl.ds(pl.multiple_of(r * SUBLANES, SUBLANES), SUBLANES), :]
        for k in range(TOP_K):
            dst = pl.multiple_of(rows_smem[r * TOP_K + k] * SUBLANES, SUBLANES)
            slab[pl.ds(dst, SUBLANES), :] = row
        return carry

    lax.fori_loop(0, tm, place, 0, unroll=4)
    send = functools.partial(_run_copies, runs_ref, hbm=x_hbm, to_hbm=True)
    send(i, slab=slab, sem=sems.at[1 + slot], wait=False)

    @pl.when(i >= 1)
    def _():
        send(i - 1, slab=slabs.at[1 - slot], sem=sems.at[2 - slot], wait=True)

    @pl.when(i == pl.num_programs(0) - 1)
    def _():
        send(i, slab=slab, sem=sems.at[1 + slot], wait=True)


def _dispatch(v, runs, rows, zero_start, n_rows):
    t = v.shape[0] // SUBLANES
    tm = COMBINE_TILE
    grid_spec = pltpu.PrefetchScalarGridSpec(
        num_scalar_prefetch=2,
        grid=(t // tm,),
        in_specs=[pl.BlockSpec((tm * SUBLANES, LANES), lambda i, *_: (i, 0)), pl.BlockSpec(memory_space=pl.ANY)],
        out_specs=pl.BlockSpec(memory_space=pl.ANY),
        scratch_shapes=[pltpu.SMEM((tm * TOP_K,), jnp.int32), pltpu.VMEM((2, TOP_K * tm * SUBLANES, LANES), F32),
                        pltpu.VMEM((MOE_TILE * SUBLANES, LANES), F32), pltpu.SemaphoreType.DMA((4,))],
    )
    return pl.pallas_call(
        _dispatch_kernel,
        grid_spec=grid_spec,
        out_shape=jax.ShapeDtypeStruct(((n_rows + MOE_TILE) * SUBLANES, LANES), F32),
        compiler_params=pltpu.CompilerParams(dimension_semantics=("arbitrary",)),
        name="dispatch",
    )(zero_start, runs, v, rows)


def _moe_kernel(be_ref, nu_ref, first_ref, x_ref, w1_ref, b1g_ref, b1l_ref, w2_ref, b2_ref, perm_ref, o_ref,
                w1g_scr, w1l_scr, w2_scr):
    i = pl.program_id(0)
    live = i < nu_ref[0]

    @pl.when(jnp.logical_and(live, first_ref[i] == 1))
    def _():
        perm = perm_ref[...]
        group = 2 * LANES
        for b in range(w1_ref.shape[1] // group):
            r = _dot(w1_ref[:, b * group:(b + 1) * group].astype(BF16), perm)
            w1g_scr[:, b * LANES:(b + 1) * LANES] = r[:, :LANES].astype(BF16)
            w1l_scr[:, b * LANES:(b + 1) * LANES] = r[:, LANES:].astype(BF16)
        w2_scr[...] = w2_ref[...].astype(BF16)

    @pl.when(live)
    def _():
        x = _unpack_rows(x_ref, MOE_TILE).astype(BF16)
        hg = _dot(x, w1g_scr[...]) + b1g_ref[...]
        hl = _dot(x, w1l_scr[...]) + b1l_ref[...]
        glu = jnp.minimum(hg, SWIGLU_LIMIT)
        lin = jnp.clip(hl, -SWIGLU_LIMIT, SWIGLU_LIMIT)
        y = glu * jax.nn.sigmoid(SWIGLU_ALPHA * glu) * (lin + 1.0)
        _pack_rows(o_ref, _dot(y.astype(BF16), w2_scr[...]) + b2_ref[...])

    @pl.when(jnp.logical_not(live))
    def _():
        o_ref[...] = jnp.zeros(o_ref.shape, F32)


def _deinterleave_perm():
    src = jnp.arange(2 * LANES)
    dst = jnp.where(src % 2 == 0, src // 2, LANES + src // 2)
    return (dst[:, None] == jnp.arange(2 * LANES)[None, :]).astype(BF16)


def _moe_experts(xb, n_rows, block_expert, n_used, first, layer, w1, b1g, b1l, w2, b2):
    d = w1.shape[2]
    tm = MOE_TILE
    dff = w2.shape[2]
    perm = _deinterleave_perm()
    tile = pl.BlockSpec((tm * SUBLANES, LANES), lambda i, be, nu, fi: (i, 0))
    wspec = lambda k, n: pl.BlockSpec((None, k, n), lambda i, be, nu, fi: (be[i], 0, 0))
    lwspec = lambda k, n: pl.BlockSpec((None, None, k, n), lambda i, be, nu, fi: (layer, be[i], 0, 0))
    grid_spec = pltpu.PrefetchScalarGridSpec(
        num_scalar_prefetch=3,
        grid=(n_rows // tm,),
        in_specs=[
            tile,
            lwspec(d, 2 * dff), wspec(1, dff), wspec(1, dff), lwspec(dff, d), wspec(1, d),
            pl.BlockSpec(perm.shape, lambda i, be, nu, fi: (0, 0)),
        ],
        out_specs=tile,
        scratch_shapes=[pltpu.VMEM((d, dff), BF16), pltpu.VMEM((d, dff), BF16), pltpu.VMEM((dff, d), BF16)],
    )
    return pl.pallas_call(
        _moe_kernel,
        grid_spec=grid_spec,
        out_shape=jax.ShapeDtypeStruct((n_rows * SUBLANES, LANES), F32),
        compiler_params=pltpu.CompilerParams(dimension_semantics=("arbitrary",), vmem_limit_bytes=MOE_VMEM_BYTES),
        name="moe_experts",
    )(block_expert, n_used, first, xb, w1, b1g, b1l, w2, b2, perm)


def _moe(v, logits, layer, w1, b1g, b1l, w2, b2):
    t = logits.shape[0]
    tm = MOE_TILE
    sel, gates, cnt, tile_cnt = _route(logits)
    counts = cnt[0, :N_EXPERTS].astype(jnp.int32)
    padded = (counts + tm - 1) // tm * tm
    pad_end = jnp.cumsum(padded)
    pad_start = pad_end - padded
    n_blocks = t * TOP_K // tm + N_EXPERTS
    n_rows = n_blocks * tm
    block_start = jnp.arange(n_blocks, dtype=jnp.int32) * tm
    block_expert = jnp.minimum(jnp.sum(block_start[:, None] >= pad_end[None, :], axis=1), N_EXPERTS - 1)
    block_expert = block_expert.astype(jnp.int32)
    n_used = (pad_end[-1] // tm).astype(jnp.int32).reshape(1)
    first = jnp.concatenate([jnp.ones((1,), jnp.int32), (block_expert[1:] != block_expert[:-1]).astype(jnp.int32)])
    n_tiles = t // COMBINE_TILE
    per_block = ROW_TILE // COMBINE_TILE
    before = tile_cnt[:, :per_block, :N_EXPERTS].astype(jnp.int32).reshape(n_tiles, N_EXPERTS)
    run_len = jnp.concatenate([before[1:], counts[None, :]], axis=0) - before
    slab_off = jnp.cumsum(run_len, axis=1) - run_len
    expert = sel[:, :TOP_K].reshape(n_tiles, COMBINE_TILE * TOP_K)
    rank = sel[:, TOP_K:2 * TOP_K].reshape(n_tiles, COMBINE_TILE * TOP_K)
    hit = expert[:, :, None] == jnp.arange(N_EXPERTS)[None, None, :]
    slab_row = rank + jnp.sum(jnp.where(hit, (slab_off - before)[:, None, :], 0), axis=-1)
    runs = jnp.concatenate([pad_start[None, :] + before, run_len], axis=1).astype(jnp.int32).reshape(-1)
    rows = slab_row.astype(jnp.int32)
    x_sorted = _dispatch(v, runs, rows, (pad_start + counts).astype(jnp.int32), n_rows)
    y_sorted = _moe_experts(x_sorted, n_rows, block_expert, n_used, first, layer, w1, b1g, b1l, w2, b2)
    return gates, runs, rows, y_sorted


def _final_kernel(runs_ref, h_ref, g2_ref, gate_ref, rows_hbm, y_hbm, g_ref, o_ref, *combine_scratch):
    moe = _moe_combine(runs_ref, rows_hbm, y_hbm, gate_ref, *combine_scratch)
    o_ref[...] = _rms(h_ref[...] + g2_ref[...] * moe, g_ref[...])


def _final_norm(h_all, pending, g, t_lat, n_lat):
    d = h_all.shape[1]
    tm = COMBINE_TILE
    g2, gates, runs, rows, y_sorted = pending
    row = pl.BlockSpec((tm, d), lambda i, *_: (i, 0))
    any_spec = pl.BlockSpec(memory_space=pl.ANY)
    grid_spec = pltpu.PrefetchScalarGridSpec(
        num_scalar_prefetch=1,
        grid=(t_lat // tm,),
        in_specs=[row, _mod_spec(n_lat // tm), pl.BlockSpec((tm, LANES), lambda i, *_: (i, 0)), any_spec, any_spec,
                  pl.BlockSpec((1, d), lambda i, *_: (0, 0))],
        out_specs=row,
        scratch_shapes=_combine_scratch(tm, d),
    )
    return pl.pallas_call(
        _final_kernel,
        grid_spec=grid_spec,
        out_shape=jax.ShapeDtypeStruct((t_lat, d), F32),
        compiler_params=pltpu.CompilerParams(dimension_semantics=("arbitrary",)),
        name="final_norm",
    )(runs, h_all, g2, gates, rows, y_sorted, g.reshape(1, d))


def _rope_tables(n_lat, n_ctx_pad):
    quarter = A_HD // 4
    inv_freq = ROPE_THETA ** (-jnp.arange(quarter, dtype=F32) / quarter)
    t = jnp.arange(n_lat)
    row = (t // GRID_W).astype(F32)
    colp = (t % GRID_W).astype(F32)
    ang_r = row[:, None] * inv_freq
    ang_c = colp[:, None] * inv_freq
    cos64 = jnp.concatenate([jnp.cos(ang_r), jnp.cos(ang_r), jnp.cos(ang_c), jnp.cos(ang_c)], axis=1)
    sin64 = jnp.concatenate([-jnp.sin(ang_r), jnp.sin(ang_r), -jnp.sin(ang_c), jnp.sin(ang_c)], axis=1)
    one, zero = jnp.ones_like(cos64), jnp.zeros_like(sin64)

    def pad(a, fill):
        return jnp.concatenate([a, jnp.full((n_ctx_pad, a.shape[1]), fill, F32)], axis=0)

    cos_a = pad(jnp.concatenate([cos64, cos64], axis=1), 1.0)
    sin_a = pad(jnp.concatenate([sin64, sin64], axis=1), 0.0)
    cos_b = pad(jnp.concatenate([cos64, one], axis=1), 1.0)
    sin_b = pad(jnp.concatenate([sin64, zero], axis=1), 0.0)
    return cos_a, sin_a, cos_b, sin_b


def _prep_uq(w_uq):
    r = w_uq.shape[0]
    w = w_uq.reshape(r, B_HEADS, B_NOPE + B_ROPE)
    w = jnp.pad(w, ((0, 0), (0, 0), (0, 2 * LANES - B_NOPE - B_ROPE)))
    return w.reshape(r, B_HEADS * 2 * LANES).astype(BF16)


def _prep_ukv(w_ukv):
    r = w_ukv.shape[0]
    w = w_ukv.reshape(r, B_HEADS, B_NOPE + B_VD)
    wk = w[:, :, :B_NOPE].reshape(r, B_HEADS * B_NOPE)
    wv = w[:, :, B_NOPE:].reshape(r, B_HEADS * B_VD)
    return wk.astype(BF16), wv.astype(BF16)


def kernel(x, c, ctx, c_ctx, norm_mix_g, norm_ffn_g, w_ada, b_ada, w_in_ab, diff_lambda, diff_subln_g, mla_q_norm_g, mla_kv_norm_g, w_uq, w_ukv, w_out_ab, w_in_c, lb_raw, hgrn_norm_g, w_out_c, w_router, b_router, w_exp1, b_exp1, w_exp2, b_exp2, final_g):
    batch, n_lat, d = x.shape
    n_ctx = ctx.shape[1]
    t_lat = batch * n_lat
    h_all = jnp.concatenate([x.reshape(t_lat, d), ctx.reshape(batch * n_ctx, d)], axis=0)

    cvec = jnp.concatenate([c, c_ctx[None, :], jnp.zeros((MOD_ROWS - batch - 1, d), F32)], axis=0)
    mods = _ada(cvec, w_ada, b_ada)

    lb_p = jax.nn.softmax(lb_raw.astype(F32), axis=0)
    lower_bounds = jnp.cumsum(lb_p, axis=0) - lb_p[0]
    tabs = _rope_tables(n_lat, PREP_TILE)

    kv_len = n_lat + n_ctx
    tq = min(FLASH_TQ, n_lat)
    tk = next(t for t in FLASH_TK_CHOICES if kv_len % t == 0)

    pending = None
    for l in range(DEPTH):
        j = l // 2
        m = mods[l]
        sh1, sc1, g1, sh2, sc2, g2 = [m[:, k * d:(k + 1) * d].reshape(MOD_ROWS, 1, d) for k in range(6)]
        wr = jnp.pad(w_router[l], ((0, 0), (0, LANES - N_EXPERTS)))
        wrh, wrl = _split_hi_lo(wr)
        br = jnp.pad(b_router[l].astype(F32), (0, LANES - N_EXPERTS)).reshape(1, LANES)
        row = lambda w: pl.BlockSpec((ROW_TILE, w), lambda i: (i, 0))
        full = lambda a: pl.BlockSpec(a.shape, lambda i: (0,) * a.ndim)
        if l % 2 == 0:
            w_in, tn = jnp.pad(w_in_ab[j], ((0, 0), (0, AB_PROJ_PAD - w_in_ab.shape[2]))).astype(BF16), AB_PROJ_PAD
        else:
            w_in, tn = w_in_c[j].astype(BF16), 1024
        p, h_all = _inproj(h_all, norm_mix_g[l], sc1, sh1, w_in, n_lat, tn, pending)
        tail = (h_all, g1, norm_ffn_g[l], sc2, sh2, wrh, wrl, br, n_lat)
        if l % 2 == 0:
            lam_init = 0.8 - 0.6 * math.exp(-0.3 * l)
            wuk, wuv = _prep_ukv(w_ukv[j])
            qa, qb, ka, va, kb, vb = _abprep(
                p, tabs, mla_q_norm_g[j].reshape(1, -1), mla_kv_norm_g[j].reshape(1, -1), _prep_uq(w_uq[j]), wuk, wuv,
                n_lat, n_ctx, batch)
            extra = (diff_lambda[j].astype(F32), diff_subln_g[j].reshape(1, A_VD))
            lat = dict(batch=batch, tq=tq, tk=tk, q_blk0=0, q_blk_stride=n_lat // tq, nq=n_lat // tq,
                       kv_blk_stride=kv_len // tk, nkv=kv_len // tk)
            oa = _flash("diff", qa, ka, va, extra, heads=A_HEADS, lam_init=lam_init, **lat)
            ob = _flash("mla", qb, kb, vb, (), heads=B_HEADS, **lat)
            if l != DEPTH - 1:
                cq = dict(batch=batch, tq=n_ctx, tk=n_ctx, q_blk0=t_lat // n_ctx, q_blk_stride=1, nq=1,
                          kv_blk_stride=kv_len // n_ctx, nkv=1)
                oa = _flash("diff", qa, ka, va, extra, heads=A_HEADS, lam_init=lam_init, out_init=oa, **cq)
                ob = _flash("mla", qb, kb, vb, (), heads=B_HEADS, out_init=ob, **cq)
            wo = w_out_ab[j].astype(BF16)
            wa, wb = wo[:A_HEADS * A_VD], wo[A_HEADS * A_VD:]
            h_mid, v_ffn, logits = _outproj(
                "ab", (oa, ob, wa, wb), [row(oa.shape[1]), row(ob.shape[1]), full(wa), full(wb)], *tail)
        else:
            p5 = p
            lb = lower_bounds[l].reshape(1, -1)
            o_f, o_b = _scan(p5, lb, n_lat, n_ctx, batch)
            ng = hgrn_norm_g[j].reshape(1, C_HD)
            wo = w_out_c[j].astype(BF16)
            gate_spec = pl.BlockSpec((ROW_TILE, d), lambda i: (i, 4))
            h_mid, v_ffn, logits = _outproj(
                "c", (o_f, o_b, p5, ng, wo), [row(d), row(d), gate_spec, full(ng), full(wo)], *tail)

        b1 = b_exp1[l].astype(F32)
        b1g, b1l = b1[:, None, 0::2], b1[:, None, 1::2]
        pending = (g2,) + _moe(v_ffn, logits, l, w_exp1, b1g, b1l, w_exp2, b_exp2[l].astype(F32)[:, None, :])
        h_all = h_mid

    return _final_norm(h_all, pending, final_g, t_lat, n_lat).reshape(batch, n_lat, d)
```

```python
import functools
import math

import jax
import jax.numpy as jnp
from jax import lax
from jax.experimental import pallas as pl
from jax.experimental.pallas import tpu as pltpu

F32 = jnp.float32
BF16 = jnp.bfloat16

D_MODEL = 1024
DEPTH = 4
GRID_W = 64
ROPE_THETA = 10000.0
NORM_EPS = 1e-6

A_HEADS = 4
A_HD = 64
A_VD = 128
A_SCALE = A_HD ** -0.5
B_HEADS = 4
B_NOPE = 128
B_ROPE = 64
B_VD = 128
B_Q_RANK = 256
B_KV_RANK = 128
B_SCALE = (B_NOPE + B_ROPE) ** -0.5
LOG2E = math.log2(math.e)
AB_PROJ_PAD = 2048

C_HEADS = 8
C_HD = 128
C_CHUNK = 64
C_SUB = 16

N_EXPERTS = 32
TOP_K = 4
SWIGLU_ALPHA = 1.702
SWIGLU_LIMIT = 7.0

LANES = 128
SUBLANES = 8
ROW_TILE = 512
PREP_TILE = 256
FLASH_SUB = 256
FLASH_TQ = 2048
FLASH_TQ_MLA = 4096
FLASH_TK_CHOICES = (2816, 768, 512, 256)
MOE_TILE = 512
COMBINE_TILE = 512
MOD_ROWS = 8
MOE_VMEM_BYTES = 60 * 1024 * 1024


def _split_hi_lo(x):
    hi = x.astype(BF16)
    lo = (x - hi.astype(F32)).astype(BF16)
    return hi, lo


def _dot(a, b):
    return jnp.dot(a, b, preferred_element_type=F32)


def _dot_nt(a, b):
    return lax.dot_general(a, b, (((1,), (1,)), ((), ())), preferred_element_type=F32)


def _dot_tn(a, b):
    return lax.dot_general(a, b, (((0,), (0,)), ((), ())), preferred_element_type=F32)


def _rms(x, g):
    ms = jnp.mean(x * x, axis=-1, keepdims=True)
    return x * lax.rsqrt(ms + NORM_EPS) * g


def _ada_kernel(c_ref, w_ref, b_ref, o_ref):
    c = c_ref[...]
    s = c * jax.nn.sigmoid(c)
    s_hi, s_lo = _split_hi_lo(s)
    w_hi, w_lo = _split_hi_lo(w_ref[...])
    o_ref[...] = _dot(s_hi, w_hi) + _dot(s_lo, w_hi) + _dot(s_hi, w_lo) + b_ref[...]


def _ada(cvec, w_ada, b_ada):
    depth, d, n6 = w_ada.shape
    tn = 1536
    return pl.pallas_call(
        _ada_kernel,
        grid=(depth, n6 // tn),
        in_specs=[
            pl.BlockSpec((MOD_ROWS, d), lambda l, j: (0, 0)),
            pl.BlockSpec((None, d, tn), lambda l, j: (l, 0, j)),
            pl.BlockSpec((None, 1, tn), lambda l, j: (l, 0, j)),
        ],
        out_specs=pl.BlockSpec((None, MOD_ROWS, tn), lambda l, j: (l, 0, j)),
        out_shape=jax.ShapeDtypeStruct((depth, MOD_ROWS, n6), F32),
        name="ada_mod",
    )(cvec, w_ada, b_ada.reshape(depth, 1, n6))


def _unpack_rows(ref, tm):
    return jnp.concatenate([ref[pl.ds(s, tm, stride=SUBLANES), :] for s in range(SUBLANES)], axis=1)


def _pack_rows(ref, x):
    tm = x.shape[0]
    for s in range(SUBLANES):
        ref[pl.ds(s, tm, stride=SUBLANES), :] = x[:, s * LANES:(s + 1) * LANES]


RUN_STRIDE = 2 * N_EXPERTS
RUN_BITS = tuple(1 << b for b in range(COMBINE_TILE.bit_length() - 1, -1, -1))


def _run_copies(runs_ref, tile, hbm, slab, sem, *, to_hbm, wait):
    base = tile * RUN_STRIDE

    def per_expert(e, cur):
        row0 = runs_ref[base + e]
        n = runs_ref[base + N_EXPERTS + e]
        pos = jnp.int32(0)
        for bit in RUN_BITS:
            take = (n & bit) != 0

            @pl.when(take)
            def _():
                h = hbm.at[pl.ds(pl.multiple_of((row0 + pos) * SUBLANES, SUBLANES), bit * SUBLANES), :]
                s = slab.at[pl.ds(pl.multiple_of((cur + pos) * SUBLANES, SUBLANES), bit * SUBLANES), :]
                cp = pltpu.make_async_copy(s, h, sem) if to_hbm else pltpu.make_async_copy(h, s, sem)
                if wait:
                    cp.wait()
                else:
                    cp.start()

            pos = pos + jnp.where(take, bit, 0)
        return cur + n

    lax.fori_loop(0, N_EXPERTS, per_expert, jnp.int32(0))


def _moe_combine(runs_ref, rows_hbm, y_hbm, gate_ref, rows_smem, slabs, ybuf, sems):
    i = pl.program_id(0)
    slot = i % 2
    tm = ybuf.shape[1] // SUBLANES
    rows_cp = pltpu.make_async_copy(rows_hbm.at[i], rows_smem, sems.at[0])
    rows_cp.start()
    fetch = functools.partial(_run_copies, runs_ref, hbm=y_hbm, to_hbm=False)

    @pl.when(i == 0)
    def _():
        fetch(0, slab=slabs.at[0], sem=sems.at[1], wait=False)

    @pl.when(i + 1 < pl.num_programs(0))
    def _():
        fetch(i + 1, slab=slabs.at[1 - slot], sem=sems.at[2 - slot], wait=False)

    fetch(i, slab=slabs.at[slot], sem=sems.at[1 + slot], wait=True)
    rows_cp.wait()
    slab = slabs.at[slot]

    def place(r, carry):
        for k in range(TOP_K):
            src = pl.multiple_of(rows_smem[r * TOP_K + k] * SUBLANES, SUBLANES)
            ybuf[k, pl.ds(pl.multiple_of(r * SUBLANES, SUBLANES), SUBLANES), :] = slab[pl.ds(src, SUBLANES), :]
        return carry

    lax.fori_loop(0, tm, place, 0, unroll=4)
    gates = gate_ref[...]
    acc = gates[:, 0:1] * _unpack_rows(ybuf.at[0], tm)
    for k in range(1, TOP_K):
        acc = acc + gates[:, k:k + 1] * _unpack_rows(ybuf.at[k], tm)
    return acc


def _inproj_kernel(h_ref, g_ref, sc_ref, sh_ref, w_ref, o_ref, u_scr):
    @pl.when(pl.program_id(1) == 0)
    def _():
        u = _rms(h_ref[...], g_ref[...]) * (1.0 + sc_ref[...]) + sh_ref[...]
        u_scr[...] = u.astype(BF16)

    o_ref[...] = _dot(u_scr[...], w_ref[...])


def _inproj_moe_kernel(runs_ref, h_ref, g2_ref, gate_ref, rows_hbm, y_hbm, g_ref, sc_ref, sh_ref, w_ref, o_ref,
                       hn_ref, u_scr, *combine_scratch):
    @pl.when(pl.program_id(1) == 0)
    def _():
        moe = _moe_combine(runs_ref, rows_hbm, y_hbm, gate_ref, *combine_scratch)
        h = h_ref[...] + g2_ref[...] * moe
        hn_ref[...] = h
        u = _rms(h, g_ref[...]) * (1.0 + sc_ref[...]) + sh_ref[...]
        u_scr[...] = u.astype(BF16)

    o_ref[...] = _dot(u_scr[...], w_ref[...])


def _mod_spec(n_lat_blocks):
    def idx(i, *_):
        return (jnp.minimum(i // n_lat_blocks, 2), 0, 0)

    return pl.BlockSpec((None, 1, D_MODEL), idx)


def _combine_scratch(tm, d):
    assert d == SUBLANES * LANES and tm == COMBINE_TILE
    return [pltpu.SMEM((tm * TOP_K,), jnp.int32), pltpu.VMEM((2, TOP_K * tm * SUBLANES, LANES), F32),
            pltpu.VMEM((TOP_K, tm * SUBLANES, LANES), F32), pltpu.SemaphoreType.DMA((3,))]


def _inproj(h_all, gain, sc, sh, w_bf, n_lat, tn, pending=None):
    t_all, d = h_all.shape
    nout = w_bf.shape[1]
    tm = COMBINE_TILE
    mod = _mod_spec(n_lat // tm)
    row = pl.BlockSpec((tm, d), lambda i, j, *_: (i, 0))
    tail_specs = [pl.BlockSpec((1, d), lambda i, j, *_: (0, 0)), mod, mod,
                  pl.BlockSpec((d, tn), lambda i, j, *_: (0, j))]
    tail = (gain.reshape(1, d), sc, sh, w_bf)
    p_spec = pl.BlockSpec((tm, tn), lambda i, j, *_: (i, j))
    p_shape = jax.ShapeDtypeStruct((t_all, nout), F32)
    params = pltpu.CompilerParams(dimension_semantics=("arbitrary", "arbitrary"))
    if pending is None:
        p = pl.pallas_call(
            _inproj_kernel,
            grid=(t_all // tm, nout // tn),
            in_specs=[row] + tail_specs,
            out_specs=p_spec,
            out_shape=p_shape,
            scratch_shapes=[pltpu.VMEM((tm, d), BF16)],
            compiler_params=params,
            name="inproj",
        )(h_all, *tail)
        return p, h_all
    g2, gates, runs, rows, y_sorted = pending
    any_spec = pl.BlockSpec(memory_space=pl.ANY)
    grid_spec = pltpu.PrefetchScalarGridSpec(
        num_scalar_prefetch=1,
        grid=(t_all // tm, nout // tn),
        in_specs=[row, mod, pl.BlockSpec((tm, LANES), lambda i, j, *_: (i, 0)), any_spec, any_spec] + tail_specs,
        out_specs=[p_spec, row],
        scratch_shapes=[pltpu.VMEM((tm, d), BF16)] + _combine_scratch(tm, d),
    )
    return pl.pallas_call(
        _inproj_moe_kernel,
        grid_spec=grid_spec,
        out_shape=[p_shape, jax.ShapeDtypeStruct((t_all, d), F32)],
        compiler_params=params,
        name="inproj_moe",
    )(runs, h_all, g2, gates, rows, y_sorted, *tail)


def _rope(x, cos, sin):
    n = x.shape[-1]
    lane = lax.broadcasted_iota(jnp.int32, x.shape, 1)
    first = (lane // 16) % 2 == 0
    partner = jnp.where(first, pltpu.roll(x, n - 16, 1), pltpu.roll(x, 16, 1))
    return x * cos + partner * sin


def _abprep_kernel(p_ref, cosa_ref, sina_ref, cosb_ref, sinb_ref, qg_ref, kvg_ref, wuq_ref, wuk_ref, wuv_ref,
                   qa_ref, qb_ref, ka_ref, va_ref, kb_ref, vb_ref):
    cosa, sina = cosa_ref[...], sina_ref[...]
    cosb, sinb = cosb_ref[...], sinb_ref[...]
    n_a = 2 * A_HEADS * A_HD
    qa = [_rope(p_ref[:, c:c + LANES], cosa, sina) * (A_SCALE * LOG2E) for c in range(0, n_a, LANES)]
    qa_ref[...] = jnp.concatenate(qa, axis=1).astype(BF16)
    ka = [_rope(p_ref[:, n_a + c:n_a + c + LANES], cosa, sina) for c in range(0, n_a, LANES)]
    ka_ref[...] = jnp.concatenate(ka, axis=1).astype(BF16)
    va_ref[...] = p_ref[:, 2 * n_a:2 * n_a + A_HEADS * A_VD].astype(BF16)

    off = 2 * n_a + A_HEADS * A_VD
    cq = _rms(p_ref[:, off:off + B_Q_RANK], qg_ref[...]).astype(BF16)
    qf = _dot(cq, wuq_ref[...])
    ckv = _rms(p_ref[:, off + B_Q_RANK:off + B_Q_RANK + B_KV_RANK], kvg_ref[...]).astype(BF16)
    kn = _dot(ckv, wuk_ref[...])
    vb_ref[...] = _dot(ckv, wuv_ref[...]).astype(BF16)
    kr_off = off + B_Q_RANK + B_KV_RANK
    krr = _rope(p_ref[:, kr_off:kr_off + LANES], cosb, sinb)
    qb, kb = [], []
    for h in range(B_HEADS):
        qb.append(qf[:, 2 * LANES * h:2 * LANES * h + LANES] * (B_SCALE * LOG2E))
        qb.append(_rope(qf[:, 2 * LANES * h + LANES:2 * LANES * (h + 1)], cosb, sinb) * (B_SCALE * LOG2E))
        kb.append(kn[:, LANES * h:LANES * (h + 1)])
        kb.append(krr)
    qb_ref[...] = jnp.concatenate(qb, axis=1).astype(BF16)
    kb_ref[...] = jnp.concatenate(kb, axis=1).astype(BF16)


def _abprep(p, tabs, qg, kvg, wuq, wuk, wuv, n_lat, n_ctx, batch):
    t_all = p.shape[0]
    tm = PREP_TILE
    nlb = n_lat // tm
    ncb = n_ctx // tm
    kvb = nlb + ncb

    def tab_idx(i):
        return (jnp.where(i < batch * nlb, i % nlb, nlb), 0)

    def kv_idx(i):
        lat = (i // nlb) * kvb + ncb + i % nlb
        j = i - batch * nlb
        ctx = (j // ncb) * kvb + j % ncb
        return (jnp.where(i < batch * nlb, lat, ctx), 0)

    tab_spec = pl.BlockSpec((tm, LANES), tab_idx)
    full = lambda a: pl.BlockSpec(a.shape, lambda i: (0,) * a.ndim)
    wq, wk, wv = 2 * A_HEADS * A_HD, B_HEADS * 2 * LANES, A_HEADS * A_VD
    kv_rows = batch * (n_lat + n_ctx)
    return pl.pallas_call(
        _abprep_kernel,
        grid=(t_all // tm,),
        in_specs=[pl.BlockSpec((tm, AB_PROJ_PAD), lambda i: (i, 0)), tab_spec, tab_spec, tab_spec, tab_spec,
                  full(qg), full(kvg), full(wuq), full(wuk), full(wuv)],
        out_specs=[
            pl.BlockSpec((tm, wq), lambda i: (i, 0)),
            pl.BlockSpec((tm, wk), lambda i: (i, 0)),
            pl.BlockSpec((tm, wq), kv_idx),
            pl.BlockSpec((tm, wv), kv_idx),
            pl.BlockSpec((tm, wk), kv_idx),
            pl.BlockSpec((tm, wv), kv_idx),
        ],
        out_shape=[
            jax.ShapeDtypeStruct((t_all, wq), BF16),
            jax.ShapeDtypeStruct((t_all, wk), BF16),
            jax.ShapeDtypeStruct((kv_rows, wq), BF16),
            jax.ShapeDtypeStruct((kv_rows, wv), BF16),
            jax.ShapeDtypeStruct((kv_rows, wk), BF16),
            jax.ShapeDtypeStruct((kv_rows, wv), BF16),
        ],
        name="ab_prep",
    )(p, *tabs, qg, kvg, wuq, wuk, wuv)


def _flash_steps(j, q_maps, k_ref, v_ref, m_refs, acc_refs):
    @pl.when(j == 0)
    def _():
        for m_ref, acc_ref in zip(m_refs, acc_refs):
            m_ref[...] = jnp.full(m_ref.shape, -jnp.inf, F32)
            acc_ref[...] = jnp.zeros(acc_ref.shape, F32)

    k = k_ref[...]
    v = v_ref[...]
    v_ext = jnp.concatenate([v, jnp.ones(v.shape, BF16)], axis=1)
    tq = m_refs[0].shape[0]
    sub = min(FLASH_SUB, tq)
    for r0 in range(0, tq, sub):
        rows = slice(r0, r0 + sub)
        for q_of, m_ref, acc_ref in zip(q_maps, m_refs, acc_refs):
            s = _dot_nt(q_of(rows), k)
            m_prev = m_ref[rows, :]
            m_new = jnp.maximum(m_prev, jnp.max(s, axis=-1, keepdims=True))
            p = jnp.exp2(s - m_new).astype(BF16)
            acc_ref[rows, :] = jnp.exp2(m_prev - m_new) * acc_ref[rows, :] + _dot(p, v_ext)
            m_ref[rows, :] = m_new


def _flash_result(acc_ref):
    wv = acc_ref.shape[1] // 2
    return acc_ref[:, :wv] / acc_ref[:, wv:]


def _flash_diff_kernel(q_ref, k_ref, v_ref, lam_ref, g_ref, o_ref, m1_ref, m2_ref, acc1_ref, acc2_ref, *, lam_init):
    j = pl.program_id(3)

    def q_half(upper):
        def get(rows):
            q = q_ref[rows, :]
            lane = lax.broadcasted_iota(jnp.int32, q.shape, 1)
            return jnp.where((lane >= A_HD) == upper, q, jnp.zeros_like(q))
        return get

    _flash_steps(j, (q_half(False), q_half(True)), k_ref, v_ref, (m1_ref, m2_ref), (acc1_ref, acc2_ref))

    @pl.when(j == pl.num_programs(3) - 1)
    def _():
        lf = lam_ref[...]
        lam = (jnp.exp(jnp.sum(lf[0:1] * lf[1:2], axis=-1, keepdims=True))
               - jnp.exp(jnp.sum(lf[2:3] * lf[3:4], axis=-1, keepdims=True)) + lam_init)
        o = _flash_result(acc1_ref) - lam * _flash_result(acc2_ref)
        o_ref[...] = (_rms(o, g_ref[...]) * (1.0 - lam_init)).astype(o_ref.dtype)


def _flash_mla_kernel(q_ref, k_ref, v_ref, o_ref, m_ref, acc_ref):
    j = pl.program_id(3)
    _flash_steps(j, (lambda rows: q_ref[rows, :],), k_ref, v_ref, (m_ref,), (acc_ref,))

    @pl.when(j == pl.num_programs(3) - 1)
    def _():
        o_ref[...] = _flash_result(acc_ref).astype(o_ref.dtype)


def _flash(kind, q, k, v, extra, *, batch, heads, tq, tk, q_blk0, q_blk_stride, nq, kv_blk_stride, nkv, lam_init=0.0,
           out_init=None):
    wq = q.shape[1] // heads
    wv = v.shape[1] // heads
    n_maps = 2 if kind == "diff" else 1
    q_spec = pl.BlockSpec((tq, wq), lambda b, h, i, j: (q_blk0 + b * q_blk_stride + i, h))
    k_spec = pl.BlockSpec((tk, wq), lambda b, h, i, j: (b * kv_blk_stride + j, h))
    v_spec = pl.BlockSpec((tk, wv), lambda b, h, i, j: (b * kv_blk_stride + j, h))
    o_spec = pl.BlockSpec((tq, wv), lambda b, h, i, j: (q_blk0 + b * q_blk_stride + i, h))
    in_specs = [q_spec, k_spec, v_spec]
    if kind == "diff":
        body = functools.partial(_flash_diff_kernel, lam_init=lam_init)
        in_specs += [pl.BlockSpec(e.shape, lambda b, h, i, j: (0, 0)) for e in extra]
    else:
        body = _flash_mla_kernel
    args = [q, k, v, *extra]
    aliases = {}
    if out_init is not None:
        n_in = len(args)
        in_specs.append(pl.BlockSpec(memory_space=pl.ANY))
        args.append(out_init)
        aliases = {n_in: 0}
        inner = body
        body = lambda *refs: inner(*refs[:n_in], *refs[n_in + 1:])
    return pl.pallas_call(
        body,
        grid=(batch, heads, nq, nkv),
        in_specs=in_specs,
        out_specs=o_spec,
        input_output_aliases=aliases,
        out_shape=jax.ShapeDtypeStruct((q.shape[0], heads * wv), BF16),
        scratch_shapes=([pltpu.VMEM((tq, 1), F32) for _ in range(n_maps)]
                        + [pltpu.VMEM((tq, 2 * wv), F32) for _ in range(n_maps)]),
        compiler_params=pltpu.CompilerParams(
            dimension_semantics=("arbitrary", "arbitrary", "arbitrary", "arbitrary")),
        name="flash_" + kind,
    )(*args)


def _out_epilogue(y, h_ref, g1_ref, nf_ref, sc2_ref, sh2_ref, wrh_ref, wrl_ref, br_ref, hn_ref, v_ref, lg_ref):
    hn = h_ref[...] + g1_ref[...] * y
    hn_ref[...] = hn
    v = _rms(hn, nf_ref[...]) * (1.0 + sc2_ref[...]) + sh2_ref[...]
    _pack_rows(v_ref, v)
    v_hi, v_lo = _split_hi_lo(v)
    wrh = wrh_ref[...]
    lg_ref[...] = _dot(v_hi, wrh) + _dot(v_lo, wrh) + _dot(v_hi, wrl_ref[...]) + br_ref[...]


def _about_kernel(oa_ref, ob_ref, wa_ref, wb_ref, *rest):
    y = _dot(oa_ref[...], wa_ref[...]) + _dot(ob_ref[...], wb_ref[...])
    _out_epilogue(y, *rest)


def _cout_kernel(of_ref, obk_ref, gate_ref, ng_ref, w_ref, *rest):
    o = of_ref[...] + obk_ref[...]
    ng = ng_ref[...]
    parts = [_rms(o[:, c:c + C_HD], ng) for c in range(0, C_HEADS * C_HD, C_HD)]
    g = gate_ref[...]
    x = jnp.concatenate(parts, axis=1) * (g * jax.nn.sigmoid(g))
    _out_epilogue(_dot(x.astype(BF16), w_ref[...]), *rest)


def _outproj(kind, ins, in_specs, h_all, g1, nf, sc2, sh2, wrh, wrl, br, n_lat):
    t_all, d = h_all.shape
    tm = ROW_TILE
    mod = _mod_spec(n_lat // tm)
    row = lambda w: pl.BlockSpec((tm, w), lambda i: (i, 0))
    full = lambda a: pl.BlockSpec(a.shape, lambda i: (0,) * a.ndim)
    nf = nf.reshape(1, d)
    return pl.pallas_call(
        _about_kernel if kind == "ab" else _cout_kernel,
        grid=(t_all // tm,),
        in_specs=in_specs + [row(d), mod, full(nf), mod, mod, full(wrh), full(wrl), full(br)],
        out_specs=[row(d), pl.BlockSpec((tm * SUBLANES, LANES), lambda i: (i, 0)), row(LANES)],
        out_shape=[
            jax.ShapeDtypeStruct((t_all, d), F32),
            jax.ShapeDtypeStruct((t_all * SUBLANES, LANES), F32),
            jax.ShapeDtypeStruct((t_all, LANES), F32),
        ],
        name="outproj_" + kind,
    )(*ins, h_all, g1, nf, sc2, sh2, wrh, wrl, br)


def _cumsum_rows(tri_bf, x):
    hi = x.astype(BF16)
    r1 = x - hi.astype(F32)
    mid = r1.astype(BF16)
    lo = (r1 - mid.astype(F32)).astype(BF16)
    return _dot(tri_bf, hi) + _dot(tri_bf, mid) + _dot(tri_bf, lo)


def _scan_kernel(qf_ref, zf_ref, vf_ref, qb_ref, zb_ref, vb_ref, lb_ref, of_ref, ob_ref, stf_ref, stb_ref):
    _scan_chunk(qf_ref, zf_ref, vf_ref, lb_ref, of_ref, stf_ref, reverse=False)
    _scan_chunk(qb_ref, zb_ref, vb_ref, lb_ref, ob_ref, stb_ref, reverse=True)


def _scan_chunk(q_ref, z_ref, v_ref, lb_ref, o_ref, st_ref, *, reverse):
    c = pl.program_id(1)

    @pl.when(c == 0)
    def _():
        st_ref[...] = jnp.zeros(st_ref.shape, F32)

    L, SB = C_CHUNK, C_SUB
    nsb = L // SB
    lb = lb_ref[...]
    f = lb + (1.0 - lb) * jax.nn.sigmoid(z_ref[...])
    kk = 1.0 - f
    lf = jnp.log(f)
    r_i = lax.broadcasted_iota(jnp.int32, (L, L), 0)
    c_i = lax.broadcasted_iota(jnp.int32, (L, L), 1)
    tri = (c_i >= r_i) if reverse else (c_i <= r_i)
    cum = _cumsum_rows(tri.astype(BF16), lf)
    last_row = 0 if reverse else L - 1
    last = cum[last_row:last_row + 1]
    q = q_ref[...]
    v = v_ref[...]
    qe = (q * jnp.exp(cum)).astype(BF16)
    kdec = (kk * jnp.exp(last - cum)).astype(BF16)
    e_last = jnp.exp(last)
    v_bf = v.astype(BF16)
    ones = jnp.ones((C_HD, C_HD), BF16)
    sub_r = lax.broadcasted_iota(jnp.int32, (SB, C_HD), 0)
    order = list(range(nsb - 1, -1, -1)) if reverse else list(range(nsb))

    outs = []
    for h in range(C_HEADS):
        hs = slice(h * C_HD, (h + 1) * C_HD)
        st = st_ref[h]
        o_h = _dot_nt(qe[:, hs], st.astype(BF16))
        st_ref[h] = st * e_last[:, hs] + _dot_tn(v_bf[:, hs], kdec[:, hs])
        cum_h, q_h, k_h, v_h = cum[:, hs], q[:, hs], kk[:, hs], v[:, hs]
        o_sub = [None] * nsb
        for p, bi in enumerate(order):
            rows = slice(bi * SB, (bi + 1) * SB)
            cum_i, q_i, k_i, v_i = cum_h[rows], q_h[rows], k_h[rows], v_h[rows]
            groups = SB // SUBLANES
            w_rows, spans = [], []
            for s in range(SB):
                g_s = s // SUBLANES
                live = range(0, g_s + 1) if reverse else range(g_s, groups)
                lo, hi = live[0] * SUBLANES, (live[-1] + 1) * SUBLANES
                ok = (sub_r[lo:hi] <= s) if reverse else (sub_r[lo:hi] >= s)
                e = jnp.where(ok, jnp.exp(cum_i[lo:hi] - cum_i[s:s + 1]), 0.0)
                w_rows.append(q_i[lo:hi] * e * k_i[s:s + 1])
                spans.append(live)
            red = _dot(jnp.concatenate(w_rows, axis=0).astype(BF16), ones)
            acc_g = [jnp.zeros((SUBLANES, C_HD), F32) for _ in range(groups)]
            off = 0
            for s in range(SB):
                for g in spans[s]:
                    acc_g[g] = acc_g[g] + red[off:off + SUBLANES] * v_i[s:s + 1]
                    off += SUBLANES
            acc = jnp.concatenate(acc_g, axis=0)
            if p > 0:
                prev = order[p - 1]
                b_row = prev * SB if reverse else prev * SB + SB - 1
                b = cum_h[b_row:b_row + 1]
                if reverse:
                    past = slice((bi + 1) * SB, L)
                else:
                    past = slice(0, bi * SB)
                qi = (q_i * jnp.exp(cum_i - b)).astype(BF16)
                kp = (k_h[past] * jnp.exp(b - cum_h[past])).astype(BF16)
                att = _dot_nt(qi, kp)
                acc = acc + _dot(att.astype(BF16), v_bf[past, hs])
            o_sub[bi] = acc
        outs.append(o_h + jnp.concatenate(o_sub, axis=0))
    o_ref[...] = jnp.concatenate(outs, axis=1)


def _scan(p5, lb, n_lat, n_ctx, batch):
    t_all = p5.shape[0]
    L = C_CHUNK
    w = C_HEADS * C_HD
    nl, nc = n_lat // L, n_ctx // L
    steps = nl + nc

    def row_idx(reverse):
        def idx(b, c):
            if reverse:
                ctx = batch * nl + b * nc + (nc - 1 - c)
                lat = b * nl + (nl - 1 - (c - nc))
            else:
                ctx = batch * nl + b * nc + c
                lat = b * nl + (c - nc)
            return jnp.where(c < nc, ctx, lat)
        return idx

    fwd, bwd = row_idx(False), row_idx(True)
    col = lambda ri, cb: pl.BlockSpec((L, w), lambda b, c: (ri(b, c), cb))
    out = jax.ShapeDtypeStruct((t_all, w), F32)
    state = pltpu.VMEM((C_HEADS, C_HD, C_HD), F32)
    return pl.pallas_call(
        _scan_kernel,
        grid=(batch, steps),
        in_specs=[col(fwd, 0), col(fwd, 1), col(fwd, 3), col(bwd, 0), col(bwd, 2), col(bwd, 3),
                  pl.BlockSpec((1, w), lambda b, c: (0, 0))],
        out_specs=[col(fwd, 0), col(bwd, 0)],
        out_shape=[out, out],
        scratch_shapes=[state, state],
        compiler_params=pltpu.CompilerParams(dimension_semantics=("arbitrary", "arbitrary")),
        name="hgrn_scan",
    )(p5, p5, p5, p5, p5, p5, lb)


def _route_kernel(lg_ref, tri_ref, sel_ref, gate_ref, cnt_ref, tb_ref, carry_ref):
    i = pl.program_id(0)

    @pl.when(i == 0)
    def _():
        carry_ref[...] = jnp.zeros(carry_ref.shape, F32)

    lg = lg_ref[...]
    lane = lax.broadcasted_iota(jnp.int32, lg.shape, 1)
    lane_f = lane.astype(F32)
    x = jnp.where(lane < N_EXPERTS, lg, -jnp.inf)
    vals, idxs, hits = [], [], []
    for _ in range(TOP_K):
        mk = jnp.max(x, axis=-1, keepdims=True)
        ik = jnp.min(jnp.where(x == mk, lane_f, float(LANES)), axis=-1, keepdims=True)
        hit = lane_f == ik
        x = jnp.where(hit, -jnp.inf, x)
        vals.append(mk)
        idxs.append(ik.astype(jnp.int32))
        hits.append(hit)
    member = hits[0] | hits[1] | hits[2] | hits[3]
    member_f = jnp.where(member, 1.0, 0.0)
    start = carry_ref[0:1]
    before = _dot(tri_ref[...], member_f.astype(BF16)) + start
    tile_rows = [start]
    for t0 in range(0, lg.shape[0] - COMBINE_TILE, COMBINE_TILE):
        tile_rows.append(tile_rows[-1] + jnp.sum(member_f[t0:t0 + COMBINE_TILE], axis=0, keepdims=True))
    tile_rows += [jnp.zeros_like(start)] * (MOD_ROWS - len(tile_rows))
    tb_ref[...] = jnp.concatenate(tile_rows, axis=0)
    carry_ref[0:1] = start + jnp.sum(member_f, axis=0, keepdims=True)
    exps = [jnp.exp(v - vals[0]) for v in vals]
    denom = exps[0] + exps[1] + exps[2] + exps[3]
    sel = jnp.zeros(lg.shape, jnp.int32)
    gate = jnp.zeros(lg.shape, F32)
    for k in range(TOP_K):
        rank = jnp.sum(jnp.where(hits[k], before, 0.0), axis=-1, keepdims=True).astype(jnp.int32)
        sel = jnp.where(lane == k, idxs[k], sel)
        sel = jnp.where(lane == TOP_K + k, rank, sel)
        gate = jnp.where(lane == k, exps[k] / denom, gate)
    sel_ref[...] = sel
    gate_ref[...] = gate
    cnt_ref[...] = carry_ref[...]


def _route(logits):
    t = logits.shape[0]
    tm = ROW_TILE
    assert tm // COMBINE_TILE <= MOD_ROWS
    tri = (jnp.arange(tm)[:, None] > jnp.arange(tm)[None, :]).astype(BF16)
    blk = pl.BlockSpec((tm, LANES), lambda i: (i, 0))
    return pl.pallas_call(
        _route_kernel,
        grid=(t // tm,),
        in_specs=[blk, pl.BlockSpec((tm, tm), lambda i: (0, 0))],
        out_specs=[blk, blk, pl.BlockSpec((MOD_ROWS, LANES), lambda i: (0, 0)),
                   pl.BlockSpec((None, MOD_ROWS, LANES), lambda i: (i, 0, 0))],
        out_shape=[
            jax.ShapeDtypeStruct((t, LANES), jnp.int32),
            jax.ShapeDtypeStruct((t, LANES), F32),
            jax.ShapeDtypeStruct((MOD_ROWS, LANES), F32),
            jax.ShapeDtypeStruct((t // tm, MOD_ROWS, LANES), F32),
        ],
        scratch_shapes=[pltpu.VMEM((MOD_ROWS, LANES), F32)],
        compiler_params=pltpu.CompilerParams(dimension_semantics=("arbitrary",)),
        name="route",
    )(logits, tri)


def _dispatch_kernel(zs_ref, runs_ref, v_ref, rows_hbm, x_hbm, rows_smem, slabs, zbuf, sems):
    i = pl.program_id(0)
    slot = i % 2
    tm = v_ref.shape[0] // SUBLANES
    pad = zbuf.shape[0]
    rows_cp = pltpu.make_async_copy(rows_hbm.at[i], rows_smem, sems.at[0])
    rows_cp.start()

    @pl.when(i == 0)
    def _():
        zbuf[...] = jnp.zeros(zbuf.shape, F32)
        for e in range(N_EXPERTS):
            z0 = pl.multiple_of(zs_ref[e] * SUBLANES, SUBLANES)
            pltpu.make_async_copy(zbuf, x_hbm.at[pl.ds(z0, pad), :], sems.at[3]).start()
        for e in range(N_EXPERTS):
            pltpu.make_async_copy(zbuf, x_hbm.at[pl.ds(0, pad), :], sems.at[3]).wait()

    rows_cp.wait()
    slab = slabs.at[slot]

    def place(r, carry):
        row = v_ref[pl.ds(pl.multiple_of(r * SUBLANES, SUBLANES), SUBLANES), :]
        for k in range(TOP_K):
            dst = pl.multiple_of(rows_smem[r * TOP_K + k] * SUBLANES, SUBLANES)
            slab[pl.ds(dst, SUBLANES), :] = row
        return carry

    lax.fori_loop(0, tm, place, 0, unroll=4)
    send = functools.partial(_run_copies, runs_ref, hbm=x_hbm, to_hbm=True)
    send(i, slab=slab, sem=sems.at[1 + slot], wait=False)

    @pl.when(i >= 1)
    def _():
        send(i - 1, slab=slabs.at[1 - slot], sem=sems.at[2 - slot], wait=True)

    @pl.when(i == pl.num_programs(0) - 1)
    def _():
        send(i, slab=slab, sem=sems.at[1 + slot], wait=True)


def _dispatch(v, runs, rows, zero_start, n_rows):
    t = v.shape[0] // SUBLANES
    tm = COMBINE_TILE
    grid_spec = pltpu.PrefetchScalarGridSpec(
        num_scalar_prefetch=2,
        grid=(t // tm,),
        in_specs=[pl.BlockSpec((tm * SUBLANES, LANES), lambda i, *_: (i, 0)), pl.BlockSpec(memory_space=pl.ANY)],
        out_specs=pl.BlockSpec(memory_space=pl.ANY),
        scratch_shapes=[pltpu.SMEM((tm * TOP_K,), jnp.int32), pltpu.VMEM((2, TOP_K * tm * SUBLANES, LANES), F32),
                        pltpu.VMEM((MOE_TILE * SUBLANES, LANES), F32), pltpu.SemaphoreType.DMA((4,))],
    )
    return pl.pallas_call(
        _dispatch_kernel,
        grid_spec=grid_spec,
        out_shape=jax.ShapeDtypeStruct(((n_rows + MOE_TILE) * SUBLANES, LANES), F32),
        compiler_params=pltpu.CompilerParams(dimension_semantics=("arbitrary",)),
        name="dispatch",
    )(zero_start, runs, v, rows)


def _moe_kernel(be_ref, nu_ref, first_ref, x_ref, w1_ref, b1g_ref, b1l_ref, w2_ref, b2_ref, perm_ref, o_ref,
                w1g_scr, w1l_scr, w2_scr):
    i = pl.program_id(0)
    live = i < nu_ref[0]

    @pl.when(jnp.logical_and(live, first_ref[i] == 1))
    def _():
        perm = perm_ref[...]
        group = 2 * LANES
        for b in range(w1_ref.shape[1] // group):
            r = _dot(w1_ref[:, b * group:(b + 1) * group].astype(BF16), perm)
            w1g_scr[:, b * LANES:(b + 1) * LANES] = r[:, :LANES].astype(BF16)
            w1l_scr[:, b * LANES:(b + 1) * LANES] = r[:, LANES:].astype(BF16)
        w2_scr[...] = w2_ref[...].astype(BF16)

    @pl.when(live)
    def _():
        x = _unpack_rows(x_ref, MOE_TILE).astype(BF16)
        hg = _dot(x, w1g_scr[...]) + b1g_ref[...]
        hl = _dot(x, w1l_scr[...]) + b1l_ref[...]
        glu = jnp.minimum(hg, SWIGLU_LIMIT)
        lin = jnp.clip(hl, -SWIGLU_LIMIT, SWIGLU_LIMIT)
        y = glu * jax.nn.sigmoid(SWIGLU_ALPHA * glu) * (lin + 1.0)
        _pack_rows(o_ref, _dot(y.astype(BF16), w2_scr[...]) + b2_ref[...])

    @pl.when(jnp.logical_not(live))
    def _():
        o_ref[...] = jnp.zeros(o_ref.shape, F32)


def _deinterleave_perm():
    src = jnp.arange(2 * LANES)
    dst = jnp.where(src % 2 == 0, src // 2, LANES + src // 2)
    return (dst[:, None] == jnp.arange(2 * LANES)[None, :]).astype(BF16)


def _moe_experts(xb, n_rows, block_expert, n_used, first, layer, w1, b1g, b1l, w2, b2):
    d = w1.shape[2]
    tm = MOE_TILE
    dff = w2.shape[2]
    perm = _deinterleave_perm()
    tile = pl.BlockSpec((tm * SUBLANES, LANES), lambda i, be, nu, fi: (i, 0))
    wspec = lambda k, n: pl.BlockSpec((None, k, n), lambda i, be, nu, fi: (be[i], 0, 0))
    lwspec = lambda k, n: pl.BlockSpec((None, None, k, n), lambda i, be, nu, fi: (layer, be[i], 0, 0))
    grid_spec = pltpu.PrefetchScalarGridSpec(
        num_scalar_prefetch=3,
        grid=(n_rows // tm,),
        in_specs=[
            tile,
            lwspec(d, 2 * dff), wspec(1, dff), wspec(1, dff), lwspec(dff, d), wspec(1, d),
            pl.BlockSpec(perm.shape, lambda i, be, nu, fi: (0, 0)),
        ],
        out_specs=tile,
        scratch_shapes=[pltpu.VMEM((d, dff), BF16), pltpu.VMEM((d, dff), BF16), pltpu.VMEM((dff, d), BF16)],
    )
    return pl.pallas_call(
        _moe_kernel,
        grid_spec=grid_spec,
        out_shape=jax.ShapeDtypeStruct((n_rows * SUBLANES, LANES), F32),
        compiler_params=pltpu.CompilerParams(dimension_semantics=("arbitrary",), vmem_limit_bytes=MOE_VMEM_BYTES),
        name="moe_experts",
    )(block_expert, n_used, first, xb, w1, b1g, b1l, w2, b2, perm)


def _moe(v, logits, layer, w1, b1g, b1l, w2, b2):
    t = logits.shape[0]
    tm = MOE_TILE
    sel, gates, cnt, tile_cnt = _route(logits)
    counts = cnt[0, :N_EXPERTS].astype(jnp.int32)
    padded = (counts + tm - 1) // tm * tm
    pad_end = jnp.cumsum(padded)
    pad_start = pad_end - padded
    n_blocks = t * TOP_K // tm + N_EXPERTS
    n_rows = n_blocks * tm
    block_start = jnp.arange(n_blocks, dtype=jnp.int32) * tm
    block_expert = jnp.minimum(jnp.sum(block_start[:, None] >= pad_end[None, :], axis=1), N_EXPERTS - 1)
    block_expert = block_expert.astype(jnp.int32)
    n_used = (pad_end[-1] // tm).astype(jnp.int32).reshape(1)
    first = jnp.concatenate([jnp.ones((1,), jnp.int32), (block_expert[1:] != block_expert[:-1]).astype(jnp.int32)])
    n_tiles = t // COMBINE_TILE
    per_block = ROW_TILE // COMBINE_TILE
    before = tile_cnt[:, :per_block, :N_EXPERTS].astype(jnp.int32).reshape(n_tiles, N_EXPERTS)
    run_len = jnp.concatenate([before[1:], counts[None, :]], axis=0) - before
    slab_off = jnp.cumsum(run_len, axis=1) - run_len
    expert = sel[:, :TOP_K].reshape(n_tiles, COMBINE_TILE * TOP_K)
    rank = sel[:, TOP_K:2 * TOP_K].reshape(n_tiles, COMBINE_TILE * TOP_K)
    hit = expert[:, :, None] == jnp.arange(N_EXPERTS)[None, None, :]
    slab_row = rank + jnp.sum(jnp.where(hit, (slab_off - before)[:, None, :], 0), axis=-1)
    runs = jnp.concatenate([pad_start[None, :] + before, run_len], axis=1).astype(jnp.int32).reshape(-1)
    rows = slab_row.astype(jnp.int32)
    x_sorted = _dispatch(v, runs, rows, (pad_start + counts).astype(jnp.int32), n_rows)
    y_sorted = _moe_experts(x_sorted, n_rows, block_expert, n_used, first, layer, w1, b1g, b1l, w2, b2)
    return gates, runs, rows, y_sorted


def _final_kernel(runs_ref, h_ref, g2_ref, gate_ref, rows_hbm, y_hbm, g_ref, o_ref, *combine_scratch):
    moe = _moe_combine(runs_ref, rows_hbm, y_hbm, gate_ref, *combine_scratch)
    o_ref[...] = _rms(h_ref[...] + g2_ref[...] * moe, g_ref[...])


def _final_norm(h_all, pending, g, t_lat, n_lat):
    d = h_all.shape[1]
    tm = COMBINE_TILE
    g2, gates, runs, rows, y_sorted = pending
    row = pl.BlockSpec((tm, d), lambda i, *_: (i, 0))
    any_spec = pl.BlockSpec(memory_space=pl.ANY)
    grid_spec = pltpu.PrefetchScalarGridSpec(
        num_scalar_prefetch=1,
        grid=(t_lat // tm,),
        in_specs=[row, _mod_spec(n_lat // tm), pl.BlockSpec((tm, LANES), lambda i, *_: (i, 0)), any_spec, any_spec,
                  pl.BlockSpec((1, d), lambda i, *_: (0, 0))],
        out_specs=row,
        scratch_shapes=_combine_scratch(tm, d),
    )
    return pl.pallas_call(
        _final_kernel,
        grid_spec=grid_spec,
        out_shape=jax.ShapeDtypeStruct((t_lat, d), F32),
        compiler_params=pltpu.CompilerParams(dimension_semantics=("arbitrary",)),
        name="final_norm",
    )(runs, h_all, g2, gates, rows, y_sorted, g.reshape(1, d))


def _rope_tables(n_lat, n_ctx_pad):
    quarter = A_HD // 4
    inv_freq = ROPE_THETA ** (-jnp.arange(quarter, dtype=F32) / quarter)
    t = jnp.arange(n_lat)
    row = (t // GRID_W).astype(F32)
    colp = (t % GRID_W).astype(F32)
    ang_r = row[:, None] * inv_freq
    ang_c = colp[:, None] * inv_freq
    cos64 = jnp.concatenate([jnp.cos(ang_r), jnp.cos(ang_r), jnp.cos(ang_c), jnp.cos(ang_c)], axis=1)
    sin64 = jnp.concatenate([-jnp.sin(ang_r), jnp.sin(ang_r), -jnp.sin(ang_c), jnp.sin(ang_c)], axis=1)
    one, zero = jnp.ones_like(cos64), jnp.zeros_like(sin64)

    def pad(a, fill):
        return jnp.concatenate([a, jnp.full((n_ctx_pad, a.shape[1]), fill, F32)], axis=0)

    cos_a = pad(jnp.concatenate([cos64, cos64], axis=1), 1.0)
    sin_a = pad(jnp.concatenate([sin64, sin64], axis=1), 0.0)
    cos_b = pad(jnp.concatenate([cos64, one], axis=1), 1.0)
    sin_b = pad(jnp.concatenate([sin64, zero], axis=1), 0.0)
    return cos_a, sin_a, cos_b, sin_b


def _prep_uq(w_uq):
    r = w_uq.shape[0]
    w = w_uq.reshape(r, B_HEADS, B_NOPE + B_ROPE)
    w = jnp.pad(w, ((0, 0), (0, 0), (0, 2 * LANES - B_NOPE - B_ROPE)))
    return w.reshape(r, B_HEADS * 2 * LANES).astype(BF16)


def _prep_ukv(w_ukv):
    r = w_ukv.shape[0]
    w = w_ukv.reshape(r, B_HEADS, B_NOPE + B_VD)
    wk = w[:, :, :B_NOPE].reshape(r, B_HEADS * B_NOPE)
    wv = w[:, :, B_NOPE:].reshape(r, B_HEADS * B_VD)
    return wk.astype(BF16), wv.astype(BF16)


def kernel(x, c, ctx, c_ctx, norm_mix_g, norm_ffn_g, w_ada, b_ada, w_in_ab, diff_lambda, diff_subln_g, mla_q_norm_g, mla_kv_norm_g, w_uq, w_ukv, w_out_ab, w_in_c, lb_raw, hgrn_norm_g, w_out_c, w_router, b_router, w_exp1, b_exp1, w_exp2, b_exp2, final_g):
    batch, n_lat, d = x.shape
    n_ctx = ctx.shape[1]
    t_lat = batch * n_lat
    h_all = jnp.concatenate([x.reshape(t_lat, d), ctx.reshape(batch * n_ctx, d)], axis=0)

    cvec = jnp.concatenate([c, c_ctx[None, :], jnp.zeros((MOD_ROWS - batch - 1, d), F32)], axis=0)
    mods = _ada(cvec, w_ada, b_ada)

    lb_p = jax.nn.softmax(lb_raw.astype(F32), axis=0)
    lower_bounds = jnp.cumsum(lb_p, axis=0) - lb_p[0]
    tabs = _rope_tables(n_lat, PREP_TILE)

    kv_len = n_lat + n_ctx
    tk = next(t for t in FLASH_TK_CHOICES if kv_len % t == 0)

    pending = None
    for l in range(DEPTH):
        j = l // 2
        m = mods[l]
        sh1, sc1, g1, sh2, sc2, g2 = [m[:, k * d:(k + 1) * d].reshape(MOD_ROWS, 1, d) for k in range(6)]
        wr = jnp.pad(w_router[l], ((0, 0), (0, LANES - N_EXPERTS)))
        wrh, wrl = _split_hi_lo(wr)
        br = jnp.pad(b_router[l].astype(F32), (0, LANES - N_EXPERTS)).reshape(1, LANES)
        row = lambda w: pl.BlockSpec((ROW_TILE, w), lambda i: (i, 0))
        full = lambda a: pl.BlockSpec(a.shape, lambda i: (0,) * a.ndim)
        if l % 2 == 0:
            w_in, tn = jnp.pad(w_in_ab[j], ((0, 0), (0, AB_PROJ_PAD - w_in_ab.shape[2]))).astype(BF16), AB_PROJ_PAD
        else:
            w_in, tn = w_in_c[j].astype(BF16), 1024
        p, h_all = _inproj(h_all, norm_mix_g[l], sc1, sh1, w_in, n_lat, tn, pending)
        tail = (h_all, g1, norm_ffn_g[l], sc2, sh2, wrh, wrl, br, n_lat)
        if l % 2 == 0:
            lam_init = 0.8 - 0.6 * math.exp(-0.3 * l)
            wuk, wuv = _prep_ukv(w_ukv[j])
            qa, qb, ka, va, kb, vb = _abprep(
                p, tabs, mla_q_norm_g[j].reshape(1, -1), mla_kv_norm_g[j].reshape(1, -1), _prep_uq(w_uq[j]), wuk, wuv,
                n_lat, n_ctx, batch)
            extra = (diff_lambda[j].astype(F32), diff_subln_g[j].reshape(1, A_VD))
            lat = lambda tq: dict(batch=batch, tq=tq, tk=tk, q_blk0=0, q_blk_stride=n_lat // tq, nq=n_lat // tq,
                                  kv_blk_stride=kv_len // tk, nkv=kv_len // tk)
            oa = _flash("diff", qa, ka, va, extra, heads=A_HEADS, lam_init=lam_init, **lat(min(FLASH_TQ, n_lat)))
            ob = _flash("mla", qb, kb, vb, (), heads=B_HEADS, **lat(min(FLASH_TQ_MLA, n_lat)))
            if l != DEPTH - 1:
                cq = dict(batch=batch, tq=n_ctx, tk=n_ctx, q_blk0=t_lat // n_ctx, q_blk_stride=1, nq=1,
                          kv_blk_stride=kv_len // n_ctx, nkv=1)
                oa = _flash("diff", qa, ka, va, extra, heads=A_HEADS, lam_init=lam_init, out_init=oa, **cq)
                ob = _flash("mla", qb, kb, vb, (), heads=B_HEADS, out_init=ob, **cq)
            wo = w_out_ab[j].astype(BF16)
            wa, wb = wo[:A_HEADS * A_VD], wo[A_HEADS * A_VD:]
            h_mid, v_ffn, logits = _outproj(
                "ab", (oa, ob, wa, wb), [row(oa.shape[1]), row(ob.shape[1]), full(wa), full(wb)], *tail)
        else:
            p5 = p
            lb = lower_bounds[l].reshape(1, -1)
            o_f, o_b = _scan(p5, lb, n_lat, n_ctx, batch)
            ng = hgrn_norm_g[j].reshape(1, C_HD)
            wo = w_out_c[j].astype(BF16)
            gate_spec = pl.BlockSpec((ROW_TILE, d), lambda i: (i, 4))
            h_mid, v_ffn, logits = _outproj(
                "c", (o_f, o_b, p5, ng, wo), [row(d), row(d), gate_spec, full(ng), full(wo)], *tail)

        b1 = b_exp1[l].astype(F32)
        b1g, b1l = b1[:, None, 0::2], b1[:, None, 1::2]
        pending = (g2,) + _moe(v_ffn, logits, l, w_exp1, b1g, b1l, w_exp2, b_exp2[l].astype(F32)[:, None, :])
        h_all = h_mid

    return _final_norm(h_all, pending, final_g, t_lat, n_lat).reshape(batch, n_lat, d)
```
